```python
import math
import jax, jax.numpy as jnp
from jax import lax
import numpy as np

D_MODEL = 2048
BATCH = 1
SEQ = 8192
DEPTH = 2

D_MIX = D_MODEL
ATT_WIDTH = D_MIX // 2
HYENA_WIDTH = D_MIX - ATT_WIDTH
HEAD_DIM = 128
N_Q_HEADS = ATT_WIDTH // HEAD_DIM
N_KV_HEADS = 2
GQA_GROUP = N_Q_HEADS // N_KV_HEADS
KV_WIDTH = N_KV_HEADS * HEAD_DIM
HYENA_ORDER = 2
D_IN = ATT_WIDTH + 2 * KV_WIDTH + (HYENA_ORDER + 1) * HYENA_WIDTH
FILTER_EMB = 33
FILTER_HIDDEN = 64
DECAY_TARGET = 1e-2
FAST_DECAY_PCT = 0.3
SLOW_DECAY_PCT = 1.5
D_FF = 5504
GRID_W = 64
Q_BLOCK = 128
ROPE_THETA = 10000.0
ROW_DIMS = HEAD_DIM // 2
COL_DIMS = HEAD_DIM - ROW_DIMS
ALPHA = (2.0 * DEPTH) ** 0.25
BETA = (8.0 * DEPTH) ** -0.25
LN_EPS = 1e-5
RMS_EPS = 1e-6

kernel_name = "hybrid_attn_hyena_deepnorm_encoder"


def layer_norm(x, g, b):
    xf = x.astype(jnp.float32)
    mu = jnp.mean(xf, axis=-1, keepdims=True)
    var = jnp.mean(jnp.square(xf - mu), axis=-1, keepdims=True)
    return ((xf - mu) * lax.rsqrt(var + LN_EPS) * g + b).astype(x.dtype)


def rms_norm(x, g):
    xf = x.astype(jnp.float32)
    ms = jnp.mean(jnp.square(xf), axis=-1, keepdims=True)
    return (xf * lax.rsqrt(ms + RMS_EPS) * g).astype(x.dtype)


def dwconv3(x, w, b):
    xp = jnp.pad(x, ((0, 0), (1, 1), (0, 0)))
    return xp[:, :-2] * w[0] + xp[:, 1:-1] * w[1] + xp[:, 2:] * w[2] + b


def rope_tables(L):
    rows = L // GRID_W
    row_pos = jnp.repeat(jnp.arange(rows, dtype=jnp.float32), GRID_W)
    col_pos = jnp.tile(jnp.arange(GRID_W, dtype=jnp.float32), rows)

    def axis_table(pos, dims):
        inv = ROPE_THETA ** (-jnp.arange(0, dims, 2, dtype=jnp.float32) / dims)
        ang = pos[:, None] * inv[None, :]
        ang = jnp.concatenate([ang, ang], axis=-1)
        return jnp.cos(ang), jnp.sin(ang)

    cos_r, sin_r = axis_table(row_pos, ROW_DIMS)
    cos_c, sin_c = axis_table(col_pos, COL_DIMS)
    return cos_r, sin_r, cos_c, sin_c


def rotate_half(x):
    x1, x2 = jnp.split(x, 2, axis=-1)
    return jnp.concatenate([-x2, x1], axis=-1)


def apply_axial_rope(x, tables):
    cos_r, sin_r, cos_c, sin_c = tables
    xr, xc = x[..., :ROW_DIMS], x[..., ROW_DIMS:]
    xr = xr * cos_r[None, :, None, :] + rotate_half(xr) * sin_r[None, :, None, :]
    xc = xc * cos_c[None, :, None, :] + rotate_half(xc) * sin_c[None, :, None, :]
    return jnp.concatenate([xr, xc], axis=-1).astype(x.dtype)


def attention_group(q, k, v, q_g, k_g, tables):
    B, L, _ = q.shape
    q = apply_axial_rope(rms_norm(q.reshape(B, L, N_Q_HEADS, HEAD_DIM), q_g), tables)
    k = apply_axial_rope(rms_norm(k.reshape(B, L, N_KV_HEADS, HEAD_DIM), k_g), tables)
    v = v.reshape(B, L, N_KV_HEADS, HEAD_DIM)
    q = q.reshape(B, L, N_KV_HEADS, GQA_GROUP, HEAD_DIM).transpose(0, 2, 3, 1, 4)
    k = k.transpose(0, 2, 1, 3)
    v = v.transpose(0, 2, 1, 3)
    nb = L // Q_BLOCK
    qb = jnp.moveaxis(q.reshape(B, N_KV_HEADS, GQA_GROUP, nb, Q_BLOCK, HEAD_DIM), 3, 0)
    scale = HEAD_DIM ** -0.5

    def block(qi):
        s = jnp.einsum('bhgqd,bhkd->bhgqk', qi, k).astype(jnp.float32) * scale
        p = jax.nn.softmax(s, axis=-1)
        return jnp.einsum('bhgqk,bhkd->bhgqd', p.astype(v.dtype), v)

    o = lax.map(block, qb)
    return o.transpose(1, 0, 4, 2, 3, 5).reshape(B, L, ATT_WIDTH)


def hyena_filters(L, w1, b1, f1, w2, b2, f2, w3, b3, decay):
    bands = (FILTER_EMB - 1) // 2
    t = jnp.linspace(0.0, 1.0, L, dtype=jnp.float32)[:, None]
    w = 2.0 * math.pi * jnp.arange(L, dtype=jnp.float32)[:, None] / L
    f = jnp.linspace(1e-4, bands - 1, bands, dtype=jnp.float32)[None, :]
    z = jnp.concatenate([t, jnp.cos(f * w), -jnp.sin(f * w)], axis=-1)
    h = jnp.sin(f1 * (z @ w1 + b1))
    h = jnp.sin(f2 * (h @ w2 + b2))
    h = (h @ w3 + b3).astype(jnp.float32).reshape(L, 2, HYENA_WIDTH)
    h = h * jnp.exp(-t[:, :, None] * jnp.abs(decay).astype(jnp.float32)[None])
    h = h / jnp.sum(jnp.abs(h), axis=(0, 1), keepdims=True)
    return h[:, 0], h[:, 1]


def bidir_fftconv(z, h_fwd, h_bwd, skip):
    L = z.shape[1]
    C = z.shape[2]
    kern = jnp.concatenate([h_fwd, jnp.zeros((1, C), jnp.float32), h_bwd[1:][::-1]], axis=0)
    K = jnp.fft.rfft(kern, n=2 * L, axis=0)
    Z = jnp.fft.rfft(z.astype(jnp.float32), n=2 * L, axis=1)
    y = jnp.fft.irfft(Z * K[None], n=2 * L, axis=1)[:, :L]
    return (y + z.astype(jnp.float32) * skip).astype(z.dtype)


def hyena_group(u, conv_w, conv_b, w1, b1, f1, w2, b2, f2, w3, b3, decay, skip):
    u = dwconv3(u, conv_w, conv_b)
    x0, x1, v = jnp.split(u, 3, axis=-1)
    h_fwd, h_bwd = hyena_filters(u.shape[1], w1, b1, f1, w2, b2, f2, w3, b3, decay)
    return x0 * bidir_fftconv(x1 * v, h_fwd, h_bwd, skip)


def setup_inputs(seed: int = 0) -> dict:
    key = jax.random.key(seed)
    ks = iter(jax.random.split(key, 48))

    def nrm(shape, scale):
        return jax.random.normal(next(ks), shape, jnp.float32) * scale

    def gain(shape):
        return 1.0 + nrm(shape, 0.1)

    s_in = D_MODEL ** -0.5
    x = nrm((BATCH, SEQ, D_MODEL), 1.0)
    ln_in_g = gain((D_MODEL,))
    ln_in_b = nrm((D_MODEL,), 0.02)
    w_q = nrm((DEPTH, D_MODEL, ATT_WIDTH), s_in)
    w_k = nrm((DEPTH, D_MODEL, KV_WIDTH), s_in)
    w_v = nrm((DEPTH, D_MODEL, KV_WIDTH), s_in * BETA)
    w_gates = nrm((DEPTH, D_MODEL, 2 * HYENA_WIDTH), s_in)
    w_hv = nrm((DEPTH, D_MODEL, HYENA_WIDTH), s_in * BETA)
    w_in = jnp.concatenate([w_q, w_k, w_v, w_gates, w_hv], axis=-1)
    q_norm_g = gain((DEPTH, HEAD_DIM))
    k_norm_g = gain((DEPTH, HEAD_DIM))
    hy_conv_w = nrm((DEPTH, 3, 3 * HYENA_WIDTH), 3 ** -0.5)
    hy_conv_b = nrm((DEPTH, 3 * HYENA_WIDTH), 0.02)
    filt_w1 = nrm((DEPTH, FILTER_EMB, FILTER_HIDDEN), FILTER_EMB ** -0.5)
    filt_b1 = nrm((DEPTH, FILTER_HIDDEN), 0.02)
    filt_f1 = gain((DEPTH, FILTER_HIDDEN))
    filt_w2 = nrm((DEPTH, FILTER_HIDDEN, FILTER_HIDDEN), FILTER_HIDDEN ** -0.5)
    filt_b2 = nrm((DEPTH, FILTER_HIDDEN), 0.02)
    filt_f2 = gain((DEPTH, FILTER_HIDDEN))
    filt_w3 = nrm((DEPTH, FILTER_HIDDEN, 2 * HYENA_WIDTH), FILTER_HIDDEN ** -0.5)
    filt_b3 = nrm((DEPTH, 2 * HYENA_WIDTH), 0.02)
    min_decay = abs(math.log(DECAY_TARGET)) / SLOW_DECAY_PCT
    max_decay = abs(math.log(DECAY_TARGET)) / FAST_DECAY_PCT
    base_decay = jnp.linspace(min_decay, max_decay, HYENA_WIDTH, dtype=jnp.float32)
    filt_decay = base_decay[None, None, :] * (1.0 + nrm((DEPTH, 2, HYENA_WIDTH), 0.05))
    hy_skip = nrm((DEPTH, HYENA_WIDTH), 0.1)
    att_out_g = gain((DEPTH, ATT_WIDTH))
    hy_out_g = gain((DEPTH, HYENA_WIDTH))
    w_out = nrm((DEPTH, D_MIX, D_MODEL), D_MIX ** -0.5 * BETA)
    b_out = nrm((DEPTH, D_MODEL), 0.02)
    ln1_g = gain((DEPTH, D_MODEL))
    ln1_b = nrm((DEPTH, D_MODEL), 0.02)
    w_up = nrm((DEPTH, D_MODEL, 2 * D_FF), s_in)
    b_up = nrm((DEPTH, 2 * D_FF), 0.02)
    ffn_conv_w = nrm((DEPTH, 3, D_FF), 3 ** -0.5)
    ffn_conv_b = nrm((DEPTH, D_FF), 0.02)
    w_down = nrm((DEPTH, D_FF, D_MODEL), D_FF ** -0.5 * BETA)
    b_down = nrm((DEPTH, D_MODEL), 0.02)
    ln2_g = gain((DEPTH, D_MODEL))
    ln2_b = nrm((DEPTH, D_MODEL), 0.02)
    return {
        "x": x, "ln_in_g": ln_in_g, "ln_in_b": ln_in_b, "w_in": w_in,
        "q_norm_g": q_norm_g, "k_norm_g": k_norm_g,
        "hy_conv_w": hy_conv_w, "hy_conv_b": hy_conv_b,
        "filt_w1": filt_w1, "filt_b1": filt_b1, "filt_f1": filt_f1,
        "filt_w2": filt_w2, "filt_b2": filt_b2, "filt_f2": filt_f2,
        "filt_w3": filt_w3, "filt_b3": filt_b3, "filt_decay": filt_decay, "hy_skip": hy_skip,
        "att_out_g": att_out_g, "hy_out_g": hy_out_g, "w_out": w_out, "b_out": b_out,
        "ln1_g": ln1_g, "ln1_b": ln1_b, "w_up": w_up, "b_up": b_up,
        "ffn_conv_w": ffn_conv_w, "ffn_conv_b": ffn_conv_b, "w_down": w_down, "b_down": b_down,
        "ln2_g": ln2_g, "ln2_b": ln2_b,
    }


def reference(x, ln_in_g, ln_in_b, w_in, q_norm_g, k_norm_g, hy_conv_w, hy_conv_b,
              filt_w1, filt_b1, filt_f1, filt_w2, filt_b2, filt_f2, filt_w3, filt_b3,
              filt_decay, hy_skip, att_out_g, hy_out_g, w_out, b_out, ln1_g, ln1_b,
              w_up, b_up, ffn_conv_w, ffn_conv_b, w_down, b_down, ln2_g, ln2_b):
    L = x.shape[1]
    tables = rope_tables(L)
    x = layer_norm(x, ln_in_g, ln_in_b)
    c_q = ATT_WIDTH
    c_k = c_q + KV_WIDTH
    c_v = c_k + KV_WIDTH
    for l in range(DEPTH):
        proj = x @ w_in[l]
        q, k, v, hy = proj[..., :c_q], proj[..., c_q:c_k], proj[..., c_k:c_v], proj[..., c_v:]
        a = rms_norm(attention_group(q, k, v, q_norm_g[l], k_norm_g[l], tables), att_out_g[l])
        h = rms_norm(hyena_group(hy, hy_conv_w[l], hy_conv_b[l], filt_w1[l], filt_b1[l], filt_f1[l],
                                 filt_w2[l], filt_b2[l], filt_f2[l], filt_w3[l], filt_b3[l],
                                 filt_decay[l], hy_skip[l]), hy_out_g[l])
        mix = jnp.concatenate([a, h], axis=-1) @ w_out[l] + b_out[l]
        x = layer_norm(ALPHA * x + mix, ln1_g[l], ln1_b[l])
        up = x @ w_up[l] + b_up[l]
        gate, val = jnp.split(up, 2, axis=-1)
        gate = dwconv3(gate, ffn_conv_w[l], ffn_conv_b[l])
        ffn = (jax.nn.gelu(gate) * val) @ w_down[l] + b_down[l]
        x = layer_norm(ALPHA * x + ffn, ln2_g[l], ln2_b[l])
    return x
```

```python
import functools
import math

import jax
import jax.numpy as jnp
from jax import lax
from jax.experimental import pallas as pl
from jax.experimental.pallas import tpu as pltpu

F32 = jnp.float32
BF16 = jnp.bfloat16

D_MODEL = 2048
SEQ = 8192
DEPTH = 2
ATT_WIDTH = 1024
HYENA_WIDTH = 1024
HEAD_DIM = 128
N_KV_HEADS = 2
GQA_GROUP = 4
KV_WIDTH = N_KV_HEADS * HEAD_DIM
ATT_IN = ATT_WIDTH + 2 * KV_WIDTH
HY_IN = 3 * HYENA_WIDTH
FILTER_HIDDEN = 64
FILTER_BANDS = 16
D_FF = 5504
D_FF_PAD = 5632
GRID_W = 64
ROPE_THETA = 10000.0
ALPHA = (2.0 * DEPTH) ** 0.25
LN_EPS = 1e-5
RMS_EPS = 1e-6

LANES = 128
BF16_SUBLANES = 16
VMEM_LIMIT = 56 * 1024 * 1024

FFT_R = 128
FFT_N = FFT_R * FFT_R
FFT_TC = 128
S_PITCH = 136
N2_CHUNK = 32
P_PITCH = 40

_dot = functools.partial(jnp.dot, preferred_element_type=F32)
_dot_hi = functools.partial(jnp.dot, preferred_element_type=F32, precision=lax.Precision.HIGHEST)


def _params(n_axes):
    return pltpu.CompilerParams(dimension_semantics=("arbitrary",) * n_axes,
                                vmem_limit_bytes=VMEM_LIMIT)


def _layer_norm(v, g, b):
    mu = jnp.mean(v, axis=-1, keepdims=True)
    d = v - mu
    var = jnp.mean(d * d, axis=-1, keepdims=True)
    return d * lax.rsqrt(var + LN_EPS) * g + b


def _rms_norm(v, g):
    ms = jnp.mean(v * v, axis=-1, keepdims=True)
    return v * lax.rsqrt(ms + RMS_EPS) * g


def _ln_kernel(x_ref, g_ref, b_ref, xf_ref, xb_ref):
    y = _layer_norm(x_ref[...], g_ref[...], b_ref[...])
    xf_ref[...] = y
    xb_ref[...] = y.astype(BF16)


def _input_ln(x, g, b, tm=512):
    L, D = x.shape
    row = pl.BlockSpec((tm, D), lambda i: (i, 0))
    vec = pl.BlockSpec((1, D), lambda i: (0, 0))
    return pl.pallas_call(
        _ln_kernel, grid=(L // tm,), in_specs=[row, vec, vec], out_specs=[row, row],
        out_shape=[jax.ShapeDtypeStruct((L, D), F32), jax.ShapeDtypeStruct((L, D), BF16)],
        compiler_params=_params(1), name="input_ln")(x, g, b)


def _mm_kernel(x_ref, w_ref, o_ref):
    o_ref[...] = _dot(x_ref[...], w_ref[...]).astype(o_ref.dtype)


def _mm_bias_kernel(x_ref, w_ref, b_ref, o_ref):
    o_ref[...] = (_dot(x_ref[...], w_ref[...]) + b_ref[...]).astype(o_ref.dtype)


def _matmul(x, w, bias=None, tm=1024, tn=512, name="matmul"):
    M, K = x.shape
    N = w.shape[1]
    in_specs = [pl.BlockSpec((tm, K), lambda i, j: (i, 0)), pl.BlockSpec((K, tn), lambda i, j: (0, j))]
    args = [x, w]
    kern = _mm_kernel
    if bias is not None:
        in_specs.append(pl.BlockSpec((1, tn), lambda i, j: (0, j)))
        args.append(bias)
        kern = _mm_bias_kernel
    return pl.pallas_call(
        kern, grid=(M // tm, N // tn), in_specs=in_specs,
        out_specs=pl.BlockSpec((tm, tn), lambda i, j: (i, j)),
        out_shape=jax.ShapeDtypeStruct((M, N), BF16),
        compiler_params=_params(2), name=name)(*args)


def _inproj_att_kernel(x_ref, w_ref, qg_ref, kg_ref, cos_ref, sa_ref, sb_ref, o_ref):
    j = pl.program_id(1)
    acc = _dot(x_ref[...], w_ref[...])
    n_q_tiles = ATT_WIDTH // (2 * HEAD_DIM)

    def norm_rope(h, g, scale):
        hn = _rms_norm(h, g)
        rot = (pltpu.roll(hn, HEAD_DIM - 32, 1) * sa_ref[...] + pltpu.roll(hn, 32, 1) * sb_ref[...])
        return (hn * cos_ref[...] + rot) * scale

    @pl.when(j < n_q_tiles)
    def _():
        for h in range(2):
            sl = slice(h * HEAD_DIM, (h + 1) * HEAD_DIM)
            o_ref[:, sl] = norm_rope(acc[:, sl], qg_ref[...], HEAD_DIM ** -0.5).astype(BF16)

    @pl.when(j == n_q_tiles)
    def _():
        for h in range(2):
            sl = slice(h * HEAD_DIM, (h + 1) * HEAD_DIM)
            o_ref[:, sl] = norm_rope(acc[:, sl], kg_ref[...], 1.0).astype(BF16)

    @pl.when(j > n_q_tiles)
    def _():
        o_ref[...] = acc.astype(BF16)


def _inproj_att(xb, w, qg, kg, cos, sin_a, sin_b, tm=1024):
    L, K = xb.shape
    tn = 2 * HEAD_DIM
    head = pl.BlockSpec((1, HEAD_DIM), lambda i, j: (0, 0))
    tab = pl.BlockSpec((tm, HEAD_DIM), lambda i, j: (i, 0))
    return pl.pallas_call(
        _inproj_att_kernel, grid=(L // tm, ATT_IN // tn),
        in_specs=[pl.BlockSpec((tm, K), lambda i, j: (i, 0)), pl.BlockSpec((K, tn), lambda i, j: (0, j)),
                  head, head, tab, tab, tab],
        out_specs=pl.BlockSpec((tm, tn), lambda i, j: (i, j)),
        out_shape=jax.ShapeDtypeStruct((L, ATT_IN), BF16),
        compiler_params=_params(2), name="inproj_att")(xb, w, qg, kg, cos, sin_a, sin_b)


def _attn_kernel(q_ref, k_ref, v_ref, o_ref, m_ref, l_ref, acc_ref):
    kb = pl.program_id(2)

    @pl.when(kb == 0)
    def _():
        m_ref[...] = jnp.full(m_ref.shape, -jnp.inf, F32)
        l_ref[...] = jnp.zeros(l_ref.shape, F32)
        acc_ref[...] = jnp.zeros(acc_ref.shape, F32)

    k = k_ref[...]
    v = v_ref[...]
    for h in range(GQA_GROUP):
        q = q_ref[:, h * HEAD_DIM:(h + 1) * HEAD_DIM]
        s = lax.dot_general(q, k, (((1,), (1,)), ((), ())), preferred_element_type=F32)
        m_prev = m_ref[h]
        m_new = jnp.maximum(m_prev, jnp.max(s, axis=1, keepdims=True))
        alpha = jnp.exp(m_prev - m_new)
        p = jnp.exp(s - m_new[:, :1])
        l_ref[h] = alpha * l_ref[h] + jnp.sum(p, axis=1, keepdims=True)
        acc_ref[h] = alpha * acc_ref[h] + _dot(p.astype(BF16), v)
        m_ref[h] = m_new

    @pl.when(kb == pl.num_programs(2) - 1)
    def _():
        for h in range(GQA_GROUP):
            o_ref[:, h * HEAD_DIM:(h + 1) * HEAD_DIM] = (acc_ref[h] / l_ref[h]).astype(BF16)


def _attention(qkv, tq=256, tk=512):
    L = qkv.shape[0]
    gw = GQA_GROUP * HEAD_DIM
    k_col0 = ATT_WIDTH // HEAD_DIM
    v_col0 = (ATT_WIDTH + KV_WIDTH) // HEAD_DIM
    return pl.pallas_call(
        _attn_kernel, grid=(N_KV_HEADS, L // tq, L // tk),
        in_specs=[pl.BlockSpec((tq, gw), lambda g, i, kb: (i, g)),
                  pl.BlockSpec((tk, HEAD_DIM), lambda g, i, kb: (kb, k_col0 + g)),
                  pl.BlockSpec((tk, HEAD_DIM), lambda g, i, kb: (kb, v_col0 + g))],
        out_specs=pl.BlockSpec((tq, gw), lambda g, i, kb: (i, g)),
        out_shape=jax.ShapeDtypeStruct((L, ATT_WIDTH), BF16),
        scratch_shapes=[pltpu.VMEM((GQA_GROUP, tq, HEAD_DIM), F32)] * 3,
        compiler_params=_params(3), name="attention")(qkv, qkv, qkv)


def _conv3_rows(u, prev_row, next_row, w_ref, b_ref):
    tm = u.shape[0]
    rid = lax.broadcasted_iota(jnp.int32, u.shape, 0)
    um = jnp.where(rid == 0, prev_row, pltpu.roll(u, 1, 0))
    up = jnp.where(rid == tm - 1, next_row, pltpu.roll(u, tm - 1, 0))
    return w_ref[0:1, :] * um + w_ref[1:2, :] * u + w_ref[2:3, :] * up + b_ref[...]


def _halo_rows(prev_ref, next_ref, i, n):
    prev_row = jnp.where(i > 0, prev_ref[...].astype(F32)[BF16_SUBLANES - 1:BF16_SUBLANES, :], 0.0)
    next_row = jnp.where(i < n - 1, next_ref[...].astype(F32)[0:1, :], 0.0)
    return prev_row, next_row


def _hy_gate_kernel(u_ref, up_ref, un_ref, w_ref, b_ref, z_ref, x0_ref):
    i = pl.program_id(0)
    prev_row, next_row = _halo_rows(up_ref, un_ref, i, pl.num_programs(0))
    c = _conv3_rows(u_ref[...].astype(F32), prev_row, next_row, w_ref, b_ref)
    C = HYENA_WIDTH
    x0_ref[...] = c[:, :C].astype(BF16)
    z_ref[...] = (c[:, C:2 * C] * c[:, 2 * C:]).astype(BF16)


def _hy_gate(u, conv_w, conv_b, tm=256):
    L, W = u.shape
    hb = tm // BF16_SUBLANES
    n_h = L // BF16_SUBLANES
    out = pl.BlockSpec((tm, HYENA_WIDTH), lambda i: (i, 0))
    return pl.pallas_call(
        _hy_gate_kernel, grid=(L // tm,),
        in_specs=[pl.BlockSpec((tm, W), lambda i: (i, 0)),
                  pl.BlockSpec((BF16_SUBLANES, W), lambda i: (jnp.maximum(i * hb - 1, 0), 0)),
                  pl.BlockSpec((BF16_SUBLANES, W), lambda i: (jnp.minimum((i + 1) * hb, n_h - 1), 0)),
                  pl.BlockSpec((3, W), lambda i: (0, 0)), pl.BlockSpec((1, W), lambda i: (0, 0))],
        out_specs=[out, out],
        out_shape=[jax.ShapeDtypeStruct((L, HYENA_WIDTH), BF16)] * 2,
        compiler_params=_params(1), name="hy_gate")(u, u, u, conv_w, conv_b)


def _filt_feat_kernel(fb_ref, w1t_ref, w1c_ref, w1s_ref, b1_ref, f1_ref, w2_ref, b2_ref, f2_ref,
                      gf_ref, gr_ref, *, tm):
    base = pl.program_id(0) * tm
    pos_b = lax.broadcasted_iota(jnp.int32, (tm, FILTER_BANDS), 0) + base
    pos_h = lax.broadcasted_iota(jnp.int32, (tm, FILTER_HIDDEN), 0) + base

    def feats(pb, ph):
        w = (2.0 * math.pi * pb.astype(F32)) / SEQ
        ang = fb_ref[...] * w
        t = ph.astype(F32) / (SEQ - 1)
        pre = (t * w1t_ref[...] + _dot_hi(jnp.cos(ang), w1c_ref[...])
               + _dot_hi(-jnp.sin(ang), w1s_ref[...]) + b1_ref[...])
        h = jnp.sin(f1_ref[...] * pre)
        return jnp.sin(f2_ref[...] * (_dot_hi(h, w2_ref[...]) + b2_ref[...]))

    gf_ref[...] = feats(pos_b, pos_h)
    gr_ref[...] = feats(jnp.where(pos_b == 0, 0, SEQ - pos_b), jnp.where(pos_h == 0, 0, SEQ - pos_h))


def _filt_feat(fband, w1, b1, f1, w2, b2, f2, tm=1024):
    H = FILTER_HIDDEN
    full = lambda a: pl.BlockSpec(a.shape, lambda i: (0,) * a.ndim)
    args = [fband, w1[0:1], w1[1:1 + FILTER_BANDS], w1[1 + FILTER_BANDS:], b1[None], f1[None], w2, b2[None], f2[None]]
    out = pl.BlockSpec((tm, H), lambda i: (i, 0))
    return pl.pallas_call(
        functools.partial(_filt_feat_kernel, tm=tm), grid=(SEQ // tm,),
        in_specs=[full(a) for a in args], out_specs=[out, out],
        out_shape=[jax.ShapeDtypeStruct((SEQ, H), F32)] * 2,
        compiler_params=_params(1), name="filt_feat")(*args)


def _filt_gen_kernel(gf_ref, gr_ref, w3f_ref, w3b_ref, b3f_ref, b3b_ref, df_ref, db_ref,
                     kf_ref, kb_ref, norm_ref, *, tm):
    i = pl.program_id(0)
    pos = lax.broadcasted_iota(jnp.int32, (tm, HYENA_WIDTH), 0) + i * tm
    pos_r = jnp.where(pos == 0, 0, SEQ - pos)
    t = pos.astype(F32) / (SEQ - 1)
    tr = pos_r.astype(F32) / (SEQ - 1)
    hf = (_dot_hi(gf_ref[...], w3f_ref[...]) + b3f_ref[...]) * jnp.exp(-t * jnp.abs(df_ref[...]))
    hb = (_dot_hi(gr_ref[...], w3b_ref[...]) + b3b_ref[...]) * jnp.exp(-tr * jnp.abs(db_ref[...]))

    @pl.when(i == 0)
    def _():
        norm_ref[...] = jnp.zeros(norm_ref.shape, F32)

    norm_ref[...] += jnp.sum(jnp.abs(hf) + jnp.abs(hb), axis=0, keepdims=True)
    kf_ref[...] = hf.astype(BF16)
    kb_ref[...] = jnp.where(pos == 0, 0.0, -hb).astype(BF16)


def _filt_gen(gf, gr, w3, b3, decay, tm=512):
    C, H = HYENA_WIDTH, FILTER_HIDDEN
    g = pl.BlockSpec((tm, H), lambda i: (i, 0))
    w = pl.BlockSpec((H, C), lambda i: (0, 0))
    vec = pl.BlockSpec((1, C), lambda i: (0, 0))
    out = pl.BlockSpec((tm, C), lambda i: (i, 0))
    return pl.pallas_call(
        functools.partial(_filt_gen_kernel, tm=tm), grid=(SEQ // tm,),
        in_specs=[g, g, w, w, vec, vec, vec, vec], out_specs=[out, out, vec],
        out_shape=[jax.ShapeDtypeStruct((SEQ, C), BF16)] * 2 + [jax.ShapeDtypeStruct((1, C), F32)],
        compiler_params=_params(1), name="filt_gen")(
            gf, gr, w3[:, :C], w3[:, C:], b3[None, :C], b3[None, C:], decay[0:1], decay[1:2])


def _fft_tables():
    N = FFT_N
    n2 = jnp.arange(FFT_R, dtype=jnp.int32)[:, None, None]
    k1 = jnp.arange(FFT_R // 2, dtype=jnp.int32)[None, :, None]
    n1 = jnp.arange(FFT_R, dtype=jnp.int32)[None, None, :]
    idx = ((FFT_R * n1 + n2) * (2 * k1 + 1)) % (2 * N)
    theta = idx.astype(F32) * (math.pi / N)
    c, s = jnp.cos(theta), jnp.sin(theta)
    ta = jnp.concatenate([c, -s], axis=1)
    ti = jnp.transpose(ta[:, :, :FFT_R // 2], (0, 2, 1)) * (2.0 / N)
    j = jnp.arange(FFT_R, dtype=jnp.int32)
    phi = ((j[:, None] * j[None, :]) % FFT_R).astype(F32) * (2.0 * math.pi / FFT_R)
    cm, sm = jnp.cos(phi), jnp.sin(phi)
    f2f = jnp.block([[cm, sm], [-sm, cm]])
    f2i = jnp.block([[cm, -sm], [sm, cm]])
    return ta.astype(BF16), ti.astype(BF16), f2f.astype(BF16), f2i.astype(BF16)


def _fft_conv_kernel(z_ref, kf_ref, kb_ref, ta_ref, ti_ref, f2f_ref, f2i_ref, y_ref, pz_ref, pk_ref, s_ref):
    half = FFT_R // 2
    tc = z_ref.shape[1]
    zeros_hi = jnp.zeros((half, tc), BF16)

    for c in range(FFT_R // N2_CHUNK):
        off = c * N2_CHUNK
        for n1 in range(half):
            pz_ref[pl.ds(n1 * P_PITCH, N2_CHUNK), :] = z_ref[pl.ds(n1 * FFT_R + off, N2_CHUNK), :].astype(F32)
            pk_ref[pl.ds(n1 * P_PITCH, N2_CHUNK), :] = kf_ref[pl.ds(n1 * FFT_R + off, N2_CHUNK), :].astype(F32)
            pk_ref[pl.ds((half + n1) * P_PITCH, N2_CHUNK), :] = (
                kb_ref[pl.ds(n1 * FFT_R + off, N2_CHUNK), :].astype(F32))

        def stage_a(j, carry):
            n2 = off + j
            xz = pz_ref[pl.ds(j, half, stride=P_PITCH), :].astype(BF16)
            xk = pk_ref[pl.ds(j, FFT_R, stride=P_PITCH), :].astype(BF16)
            rhs = jnp.concatenate([jnp.concatenate([xz, zeros_hi], axis=0), xk], axis=1)
            a = _dot(ta_ref[n2], rhs)
            row = pl.multiple_of(n2 * S_PITCH, 8)
            s_ref[0, pl.ds(row, FFT_R), :] = a[:, :tc]
            s_ref[1, pl.ds(row, FFT_R), :] = a[:, tc:]
            return carry

        lax.fori_loop(0, N2_CHUNK, stage_a, 0)

    def stage_b(k1, carry):
        re_rows = pl.ds(k1, FFT_R, stride=S_PITCH)
        im_rows = pl.ds(half + k1, FFT_R, stride=S_PITCH)
        rhs = jnp.concatenate(
            [jnp.concatenate([s_ref[0, re_rows, :], s_ref[1, re_rows, :]], axis=1),
             jnp.concatenate([s_ref[0, im_rows, :], s_ref[1, im_rows, :]], axis=1)], axis=0).astype(BF16)
        x = _dot(f2f_ref[...], rhs)
        zr, zi = x[:FFT_R, :tc], x[FFT_R:, :tc]
        kr, ki = x[:FFT_R, tc:], x[FFT_R:, tc:]
        prod = jnp.concatenate([zr * kr - zi * ki, zr * ki + zi * kr], axis=0).astype(BF16)
        d = _dot(f2i_ref[...], prod)
        s_ref[0, re_rows, :] = d[:FFT_R]
        s_ref[0, im_rows, :] = d[FFT_R:]
        return carry

    lax.fori_loop(0, half, stage_b, 0)

    def stage_out(n2, carry):
        d = s_ref[0, pl.ds(pl.multiple_of(n2 * S_PITCH, 8), FFT_R), :].astype(BF16)
        y_ref[pl.ds(n2, half, stride=FFT_R), :] = _dot(ti_ref[n2], d)
        return carry

    lax.fori_loop(0, FFT_R, stage_out, 0)


def _fft_conv(z, kf, kb, tables):
    L, C = z.shape
    tc = FFT_TC
    ta, ti, f2f, f2i = tables
    col = pl.BlockSpec((L, tc), lambda j: (0, j))
    full = lambda a: pl.BlockSpec(a.shape, lambda j: (0,) * a.ndim)
    half = FFT_R // 2
    return pl.pallas_call(
        _fft_conv_kernel, grid=(C // tc,),
        in_specs=[col, col, col, full(ta), full(ti), full(f2f), full(f2i)],
        out_specs=col, out_shape=jax.ShapeDtypeStruct((L, C), F32),
        scratch_shapes=[pltpu.VMEM((half * P_PITCH, tc), F32), pltpu.VMEM((FFT_R * P_PITCH, tc), F32),
                        pltpu.VMEM((2, FFT_R * S_PITCH, tc), F32)],
        compiler_params=_params(1), name="fft_conv")(z, kf, kb, ta, ti, f2f, f2i)


def _outproj_kernel(a_ref, x0_ref, z_ref, yc_ref, norm_ref, skip_ref, ag_ref, hg_ref,
                    wa_ref, wh_ref, b_ref, x_ref, g_ref, beta_ref, xf_ref, xb_ref):
    a = _rms_norm(a_ref[...].astype(F32), ag_ref[...])
    z = z_ref[...].astype(F32)
    hy = x0_ref[...].astype(F32) * (yc_ref[...] * (1.0 / norm_ref[...]) + skip_ref[...] * z)
    h = _rms_norm(hy, hg_ref[...])
    mix = _dot(a.astype(BF16), wa_ref[...]) + _dot(h.astype(BF16), wh_ref[...]) + b_ref[...]
    y = _layer_norm(ALPHA * x_ref[...] + mix, g_ref[...], beta_ref[...])
    xf_ref[...] = y
    xb_ref[...] = y.astype(BF16)


def _outproj(a, x0, z, yc, norm, skip, ag, hg, w_out, b_out, x, g, beta, tm=256):
    L, D = x.shape
    C = HYENA_WIDTH
    half = pl.BlockSpec((tm, C), lambda i: (i, 0))
    hvec = pl.BlockSpec((1, C), lambda i: (0, 0))
    row = pl.BlockSpec((tm, D), lambda i: (i, 0))
    vec = pl.BlockSpec((1, D), lambda i: (0, 0))
    wa = pl.BlockSpec((ATT_WIDTH, D), lambda i: (0, 0))
    wh = pl.BlockSpec((C, D), lambda i: (1, 0))
    return pl.pallas_call(
        _outproj_kernel, grid=(L // tm,),
        in_specs=[half, half, half, half, hvec, hvec, hvec, hvec, wa, wh, vec, row, vec, vec],
        out_specs=[row, row],
        out_shape=[jax.ShapeDtypeStruct((L, D), F32), jax.ShapeDtypeStruct((L, D), BF16)],
        compiler_params=_params(1), name="outproj_ln")(
            a, x0, z, yc, norm, skip, ag, hg, w_out, w_out, b_out, x, g, beta)


def _gelu_tanh(x):
    return x * (0.5 * (1.0 + jnp.tanh(math.sqrt(2.0 / math.pi) * (x + 0.044715 * (x * x * x)))))


def _ffn_down_kernel(gate_ref, gp_ref, gn_ref, val_ref, cw_ref, cb_ref, w_ref, b_ref, x_ref, g_ref, beta_ref,
                     xf_ref, xb_ref, acc_ref):
    i = pl.program_id(0)
    k = pl.program_id(1)
    prev_row, next_row = _halo_rows(gp_ref, gn_ref, i, pl.num_programs(0))
    gate = _conv3_rows(gate_ref[...].astype(F32), prev_row, next_row, cw_ref, cb_ref)
    h = (_gelu_tanh(gate) * val_ref[...].astype(F32)).astype(BF16)
    part = _dot(h, w_ref[...])

    @pl.when(k == 0)
    def _():
        acc_ref[...] = part

    @pl.when(k > 0)
    def _():
        acc_ref[...] += part

    @pl.when(k == pl.num_programs(1) - 1)
    def _():
        y = _layer_norm(ALPHA * x_ref[...] + acc_ref[...] + b_ref[...], g_ref[...], beta_ref[...])
        xf_ref[...] = y
        xb_ref[...] = y.astype(BF16)


def _ffn_down(up, conv_w, conv_b, w_down, b_down, x, g, beta, tm=512, tk=512):
    L, D = x.shape
    hb = tm // BF16_SUBLANES
    n_h = L // BF16_SUBLANES
    nk = D_FF_PAD // tk
    row = pl.BlockSpec((tm, D), lambda i, k: (i, 0))
    vec = pl.BlockSpec((1, D), lambda i, k: (0, 0))
    return pl.pallas_call(
        _ffn_down_kernel, grid=(L // tm, nk),
        in_specs=[pl.BlockSpec((tm, tk), lambda i, k: (i, k)),
                  pl.BlockSpec((BF16_SUBLANES, tk), lambda i, k: (jnp.maximum(i * hb - 1, 0), k)),
                  pl.BlockSpec((BF16_SUBLANES, tk), lambda i, k: (jnp.minimum((i + 1) * hb, n_h - 1), k)),
                  pl.BlockSpec((tm, tk), lambda i, k: (i, nk + k)),
                  pl.BlockSpec((3, tk), lambda i, k: (0, k)), pl.BlockSpec((1, tk), lambda i, k: (0, k)),
                  pl.BlockSpec((tk, D), lambda i, k: (k, 0)), vec, row, vec, vec],
        out_specs=[row, row],
        out_shape=[jax.ShapeDtypeStruct((L, D), F32), jax.ShapeDtypeStruct((L, D), BF16)],
        scratch_shapes=[pltpu.VMEM((tm, D), F32)],
        compiler_params=_params(2), name="ffn_down_ln")(
            up, up, up, up, conv_w, conv_b, w_down, b_down, x, g, beta)


def _rope_tables(L):
    rows = L // GRID_W
    row_pos = jnp.repeat(jnp.arange(rows, dtype=F32), GRID_W)
    col_pos = jnp.tile(jnp.arange(GRID_W, dtype=F32), rows)
    dims = HEAD_DIM // 2

    def axis_table(pos):
        inv = ROPE_THETA ** (-jnp.arange(0, dims, 2, dtype=F32) / dims)
        ang = pos[:, None] * inv[None, :]
        ang = jnp.concatenate([ang, ang], axis=-1)
        return jnp.cos(ang), jnp.sin(ang)

    cos_r, sin_r = axis_table(row_pos)
    cos_c, sin_c = axis_table(col_pos)
    cos = jnp.concatenate([cos_r, cos_c], axis=-1)
    sin = jnp.concatenate([sin_r, sin_c], axis=-1)
    first = (jnp.arange(HEAD_DIM) % dims) < dims // 2
    return cos, jnp.where(first, -sin, 0.0), jnp.where(first, 0.0, sin)


def _pad_ff(a, axis):
    pad = [(0, 0)] * a.ndim
    pad[axis] = (0, D_FF_PAD - D_FF)
    return jnp.pad(a, pad)


def kernel(x, ln_in_g, ln_in_b, w_in, q_norm_g, k_norm_g, hy_conv_w, hy_conv_b, filt_w1, filt_b1, filt_f1, filt_w2, filt_b2, filt_f2, filt_w3, filt_b3, filt_decay, hy_skip, att_out_g, hy_out_g, w_out, b_out, ln1_g, ln1_b, w_up, b_up, ffn_conv_w, ffn_conv_b, w_down, b_down, ln2_g, ln2_b):
    assert x.shape == (1, SEQ, D_MODEL) and w_in.shape == (DEPTH, D_MODEL, ATT_IN + HY_IN)
    cos, sin_a, sin_b = _rope_tables(SEQ)
    tables = _fft_tables()
    fband = jnp.linspace(1e-4, FILTER_BANDS - 1, FILTER_BANDS, dtype=F32)[None, :]

    xf, xb = _input_ln(x[0], ln_in_g[None], ln_in_b[None])
    for l in range(DEPTH):
        w_in_b = w_in[l].astype(BF16)
        qkv = _inproj_att(xb, w_in_b[:, :ATT_IN], q_norm_g[l][None], k_norm_g[l][None], cos, sin_a, sin_b)
        att = _attention(qkv)
        u = _matmul(xb, w_in_b[:, ATT_IN:], name="inproj_hy")
        z, x0 = _hy_gate(u, hy_conv_w[l], hy_conv_b[l][None])
        gf, gr = _filt_feat(fband, filt_w1[l], filt_b1[l], filt_f1[l], filt_w2[l], filt_b2[l], filt_f2[l])
        kf, kb, norm = _filt_gen(gf, gr, filt_w3[l], filt_b3[l], filt_decay[l])
        yc = _fft_conv(z, kf, kb, tables)
        xf, xb = _outproj(att, x0, z, yc, norm, hy_skip[l][None], att_out_g[l][None], hy_out_g[l][None],
                          w_out[l].astype(BF16), b_out[l][None], xf, ln1_g[l][None], ln1_b[l][None])
        w_up_b = jnp.concatenate([_pad_ff(w_up[l][:, :D_FF], 1), _pad_ff(w_up[l][:, D_FF:], 1)], axis=1).astype(BF16)
        b_up_p = jnp.concatenate([_pad_ff(b_up[l][:D_FF], 0), _pad_ff(b_up[l][D_FF:], 0)])[None]
        up = _matmul(xb, w_up_b, b_up_p, name="ffn_up")
        xf, xb = _ffn_down(up, _pad_ff(ffn_conv_w[l], 1), _pad_ff(ffn_conv_b[l], 0)[None],
                           _pad_ff(w_down[l], 0).astype(BF16), b_down[l][None], xf, ln2_g[l][None], ln2_b[l][None])
    return xf[None]
```

```python
import functools
import math

import jax
import jax.numpy as jnp
from jax import lax
from jax.experimental import pallas as pl
from jax.experimental.pallas import tpu as pltpu

F32 = jnp.float32
BF16 = jnp.bfloat16

D_MODEL = 2048
SEQ = 8192
DEPTH = 2
ATT_WIDTH = 1024
HYENA_WIDTH = 1024
HEAD_DIM = 128
N_KV_HEADS = 2
GQA_GROUP = 4
KV_WIDTH = N_KV_HEADS * HEAD_DIM
ATT_IN = ATT_WIDTH + 2 * KV_WIDTH
HY_IN = 3 * HYENA_WIDTH
FILTER_HIDDEN = 64
FILTER_BANDS = 16
D_FF = 5504
D_FF_PAD = 5632
GRID_W = 64
ROPE_THETA = 10000.0
ALPHA = (2.0 * DEPTH) ** 0.25
LN_EPS = 1e-5
RMS_EPS = 1e-6

LANES = 128
BF16_SUBLANES = 16
VMEM_LIMIT = 56 * 1024 * 1024

FFT_R = 128
FFT_N = FFT_R * FFT_R
FFT_TC = 128
S_PITCH = 136
N2_CHUNK = 32
P_PITCH = 40
FFT_UNROLL = 8
FFT_KB = 4

_dot = functools.partial(jnp.dot, preferred_element_type=F32)
_dot_hi = functools.partial(jnp.dot, preferred_element_type=F32, precision=lax.Precision.HIGHEST)


def _params(n_axes):
    return pltpu.CompilerParams(dimension_semantics=("arbitrary",) * n_axes,
                                vmem_limit_bytes=VMEM_LIMIT)


def _layer_norm(v, g, b):
    mu = jnp.mean(v, axis=-1, keepdims=True)
    d = v - mu
    var = jnp.mean(d * d, axis=-1, keepdims=True)
    return d * lax.rsqrt(var + LN_EPS) * g + b


def _rms_norm(v, g):
    ms = jnp.mean(v * v, axis=-1, keepdims=True)
    return v * lax.rsqrt(ms + RMS_EPS) * g


def _ln_kernel(x_ref, g_ref, b_ref, xf_ref, xb_ref):
    y = _layer_norm(x_ref[...], g_ref[...], b_ref[...])
    xf_ref[...] = y
    xb_ref[...] = y.astype(BF16)


def _input_ln(x, g, b, tm=512):
    L, D = x.shape
    row = pl.BlockSpec((tm, D), lambda i: (i, 0))
    vec = pl.BlockSpec((1, D), lambda i: (0, 0))
    return pl.pallas_call(
        _ln_kernel, grid=(L // tm,), in_specs=[row, vec, vec], out_specs=[row, row],
        out_shape=[jax.ShapeDtypeStruct((L, D), F32), jax.ShapeDtypeStruct((L, D), BF16)],
        compiler_params=_params(1), name="input_ln")(x, g, b)


def _inproj_head_kernel(x_ref, w_ref, g_ref, cos_ref, sa_ref, sb_ref, o_ref, *, scale, transpose_out):
    acc = _dot(x_ref[...], w_ref[...])
    for h in range(2):
        sl = slice(h * HEAD_DIM, (h + 1) * HEAD_DIM)
        hn = _rms_norm(acc[:, sl], g_ref[...])
        rot = pltpu.roll(hn, HEAD_DIM - 32, 1) * sa_ref[...] + pltpu.roll(hn, 32, 1) * sb_ref[...]
        r = (hn * cos_ref[...] + rot) * scale
        if transpose_out:
            o_ref[sl, :] = r.T.astype(BF16)
        else:
            o_ref[:, sl] = r.astype(BF16)


def _inproj_vt_kernel(x_ref, w_ref, o_ref):
    o_ref[...] = _dot(x_ref[...], w_ref[...]).T.astype(BF16)


def _inproj_att(xb, w_in_b, qg, kg, cos, sin_a, sin_b, tm=1024):
    L, K = xb.shape
    tn = 2 * HEAD_DIM
    n_q = ATT_WIDTH // tn
    x_spec = pl.BlockSpec((tm, K), lambda i, j: (i, 0))
    head = pl.BlockSpec((1, HEAD_DIM), lambda i, j: (0, 0))
    tab = pl.BlockSpec((tm, HEAD_DIM), lambda i, j: (i, 0))
    q_scale = HEAD_DIM ** -0.5 * math.log2(math.e)
    qt = pl.pallas_call(
        functools.partial(_inproj_head_kernel, scale=q_scale, transpose_out=True), grid=(L // tm, n_q),
        in_specs=[x_spec, pl.BlockSpec((K, tn), lambda i, j: (0, j)), head, tab, tab, tab],
        out_specs=pl.BlockSpec((tn, tm), lambda i, j: (j, i)),
        out_shape=jax.ShapeDtypeStruct((ATT_WIDTH, L), BF16),
        compiler_params=_params(2), name="inproj_q")(xb, w_in_b, qg, cos, sin_a, sin_b)
    k = pl.pallas_call(
        functools.partial(_inproj_head_kernel, scale=1.0, transpose_out=False), grid=(L // tm, 1),
        in_specs=[x_spec, pl.BlockSpec((K, tn), lambda i, j: (0, n_q)), head, tab, tab, tab],
        out_specs=pl.BlockSpec((tm, tn), lambda i, j: (i, 0)),
        out_shape=jax.ShapeDtypeStruct((L, KV_WIDTH), BF16),
        compiler_params=_params(2), name="inproj_k")(xb, w_in_b, kg, cos, sin_a, sin_b)
    vt = pl.pallas_call(
        _inproj_vt_kernel, grid=(L // tm, 1),
        in_specs=[x_spec, pl.BlockSpec((K, tn), lambda i, j: (0, n_q + 1))],
        out_specs=pl.BlockSpec((tn, tm), lambda i, j: (0, i)),
        out_shape=jax.ShapeDtypeStruct((KV_WIDTH, L), BF16),
        compiler_params=_params(2), name="inproj_v")(xb, w_in_b)
    return qt, k, vt


def _attn_kernel(qt_ref, k_ref, vt_ref, o_ref, m_ref, l_ref, acc_ref):
    kb = pl.program_id(2)

    @pl.when(kb == 0)
    def _():
        m_ref[...] = jnp.full(m_ref.shape, -jnp.inf, F32)
        l_ref[...] = jnp.zeros(l_ref.shape, F32)
        acc_ref[...] = jnp.zeros(acc_ref.shape, F32)

    k = k_ref[...]
    vt = vt_ref[...]
    scores = [_dot(k, qt_ref[h * HEAD_DIM:(h + 1) * HEAD_DIM, :]) for h in range(GQA_GROUP)]
    for h in range(GQA_GROUP):
        s = scores[h]
        m_prev = m_ref[h]
        m_new = jnp.maximum(m_prev, jnp.max(s, axis=0, keepdims=True))
        alpha = jnp.exp2(m_prev - m_new)
        p = jnp.exp2(s - m_new)
        l_ref[h] = alpha * l_ref[h] + jnp.sum(p, axis=0, keepdims=True)
        acc_ref[h] = alpha * acc_ref[h] + _dot(vt, p.astype(BF16))
        m_ref[h] = m_new

    @pl.when(kb == pl.num_programs(2) - 1)
    def _():
        for h in range(GQA_GROUP):
            o_ref[:, h * HEAD_DIM:(h + 1) * HEAD_DIM] = (acc_ref[h] / l_ref[h]).T.astype(BF16)


def _attention(qt, k, vt, tq=256, tk=1024):
    L = k.shape[0]
    gw = GQA_GROUP * HEAD_DIM
    return pl.pallas_call(
        _attn_kernel, grid=(N_KV_HEADS, L // tq, L // tk),
        in_specs=[pl.BlockSpec((gw, tq), lambda g, i, kb: (g, i)),
                  pl.BlockSpec((tk, HEAD_DIM), lambda g, i, kb: (kb, g)),
                  pl.BlockSpec((HEAD_DIM, tk), lambda g, i, kb: (g, kb))],
        out_specs=pl.BlockSpec((tq, gw), lambda g, i, kb: (i, g)),
        out_shape=jax.ShapeDtypeStruct((L, ATT_WIDTH), BF16),
        scratch_shapes=[pltpu.VMEM((GQA_GROUP, 1, tq), F32), pltpu.VMEM((GQA_GROUP, 1, tq), F32),
                        pltpu.VMEM((GQA_GROUP, HEAD_DIM, tq), F32)],
        compiler_params=_params(3), name="attention")(qt, k, vt)


HALO = BF16_SUBLANES


def _halo_specs(tm, K, L):
    hb = tm // HALO
    n_h = L // HALO
    return [pl.BlockSpec((tm, K), lambda i, j: (i, 0)),
            pl.BlockSpec((HALO, K), lambda i, j: (jnp.maximum(i * hb - 1, 0), 0)),
            pl.BlockSpec((HALO, K), lambda i, j: (jnp.minimum((i + 1) * hb, n_h - 1), 0))]


def _fill_extended(xe_ref, x_ref, xp_ref, xn_ref):
    tm = x_ref.shape[0]

    @pl.when(pl.program_id(1) == 0)
    def _():
        xe_ref[0:HALO, :] = xp_ref[...]
        xe_ref[HALO:HALO + tm, :] = x_ref[...]
        xe_ref[HALO + tm:, :] = xn_ref[...]


def _conv3_extended(u, w_ref, b_ref):
    n_ext = u.shape[0]
    tm = n_ext - 2 * HALO
    i = pl.program_id(0)
    top = u[:HALO] * jnp.where(i > 0, 1.0, 0.0)
    bot = u[HALO + tm:] * jnp.where(i < pl.num_programs(0) - 1, 1.0, 0.0)
    u = jnp.concatenate([top, u[HALO:HALO + tm], bot], axis=0)
    um = pltpu.roll(u, 1, 0)[HALO:HALO + tm]
    up = pltpu.roll(u, n_ext - 1, 0)[HALO:HALO + tm]
    return w_ref[0:1, :] * um + w_ref[1:2, :] * u[HALO:HALO + tm] + w_ref[2:3, :] * up + b_ref[...]


def _inproj_hy_kernel(x_ref, xp_ref, xn_ref, w0_ref, w1_ref, w2_ref, cw0_ref, cw1_ref, cw2_ref,
                      cb0_ref, cb1_ref, cb2_ref, z_ref, x0_ref, xe_ref):
    _fill_extended(xe_ref, x_ref, xp_ref, xn_ref)
    xe = xe_ref[...]
    x0_ref[...] = _conv3_extended(_dot(xe, w0_ref[...]), cw0_ref, cb0_ref).astype(BF16)
    x1 = _conv3_extended(_dot(xe, w1_ref[...]), cw1_ref, cb1_ref)
    v = _conv3_extended(_dot(xe, w2_ref[...]), cw2_ref, cb2_ref)
    z_ref[...] = (x1 * v).astype(BF16)


def _inproj_hy(xb, w_in_b, conv_w, conv_b, tm=1024, tn=256):
    L, K = xb.shape
    C = HYENA_WIDTH
    nj = C // tn
    w_specs = [pl.BlockSpec((K, tn), functools.partial(lambda i, j, g: (0, (ATT_IN + g * C) // tn + j), g=g))
               for g in range(3)]
    cw_specs = [pl.BlockSpec((3, tn), functools.partial(lambda i, j, g: (0, g * nj + j), g=g)) for g in range(3)]
    cb_specs = [pl.BlockSpec((1, tn), functools.partial(lambda i, j, g: (0, g * nj + j), g=g)) for g in range(3)]
    out = pl.BlockSpec((tm, tn), lambda i, j: (i, j))
    return pl.pallas_call(
        _inproj_hy_kernel, grid=(L // tm, nj),
        in_specs=_halo_specs(tm, K, L) + w_specs + cw_specs + cb_specs,
        out_specs=[out, out], out_shape=[jax.ShapeDtypeStruct((L, C), BF16)] * 2,
        scratch_shapes=[pltpu.VMEM((tm + 2 * HALO, K), BF16)],
        compiler_params=_params(2), name="inproj_hy")(
            xb, xb, xb, w_in_b, w_in_b, w_in_b, conv_w, conv_w, conv_w, conv_b, conv_b, conv_b)


def _filt_feat_kernel(fb_ref, w1t_ref, w1c_ref, w1s_ref, b1_ref, f1_ref, w2_ref, b2_ref, f2_ref,
                      gf_ref, gr_ref, *, tm):
    base = pl.program_id(0) * tm
    pos_b = lax.broadcasted_iota(jnp.int32, (tm, FILTER_BANDS), 0) + base
    pos_h = lax.broadcasted_iota(jnp.int32, (tm, FILTER_HIDDEN), 0) + base

    def feats(pb, ph):
        w = (2.0 * math.pi * pb.astype(F32)) / SEQ
        ang = fb_ref[...] * w
        t = ph.astype(F32) / (SEQ - 1)
        pre = (t * w1t_ref[...] + _dot_hi(jnp.cos(ang), w1c_ref[...])
               + _dot_hi(-jnp.sin(ang), w1s_ref[...]) + b1_ref[...])
        h = jnp.sin(f1_ref[...] * pre)
        return jnp.sin(f2_ref[...] * (_dot_hi(h, w2_ref[...]) + b2_ref[...]))

    gf_ref[...] = feats(pos_b, pos_h)
    gr_ref[...] = feats(jnp.where(pos_b == 0, 0, SEQ - pos_b), jnp.where(pos_h == 0, 0, SEQ - pos_h))


def _filt_feat(fband, w1, b1, f1, w2, b2, f2, tm=1024):
    H = FILTER_HIDDEN
    full = lambda a: pl.BlockSpec(a.shape, lambda i: (0,) * a.ndim)
    args = [fband, w1[0:1], w1[1:1 + FILTER_BANDS], w1[1 + FILTER_BANDS:], b1[None], f1[None], w2, b2[None], f2[None]]
    out = pl.BlockSpec((tm, H), lambda i: (i, 0))
    return pl.pallas_call(
        functools.partial(_filt_feat_kernel, tm=tm), grid=(SEQ // tm,),
        in_specs=[full(a) for a in args], out_specs=[out, out],
        out_shape=[jax.ShapeDtypeStruct((SEQ, H), F32)] * 2,
        compiler_params=_params(1), name="filt_feat")(*args)


def _filt_gen_kernel(gf_ref, gr_ref, w3f_ref, w3b_ref, b3f_ref, b3b_ref, df_ref, db_ref,
                     kf_ref, kb_ref, norm_ref, *, tm):
    i = pl.program_id(0)
    pos = lax.broadcasted_iota(jnp.int32, (tm, HYENA_WIDTH), 0) + i * tm
    pos_r = jnp.where(pos == 0, 0, SEQ - pos)
    t = pos.astype(F32) / (SEQ - 1)
    tr = pos_r.astype(F32) / (SEQ - 1)
    hf = (_dot_hi(gf_ref[...], w3f_ref[...]) + b3f_ref[...]) * jnp.exp(-t * jnp.abs(df_ref[...]))
    hb = (_dot_hi(gr_ref[...], w3b_ref[...]) + b3b_ref[...]) * jnp.exp(-tr * jnp.abs(db_ref[...]))

    @pl.when(i == 0)
    def _():
        norm_ref[...] = jnp.zeros(norm_ref.shape, F32)

    norm_ref[...] += jnp.sum(jnp.abs(hf) + jnp.abs(hb), axis=0, keepdims=True)
    kf_ref[...] = hf.astype(BF16)
    kb_ref[...] = jnp.where(pos == 0, 0.0, -hb).astype(BF16)


def _filt_gen(gf, gr, w3, b3, decay, tm=512):
    C, H = HYENA_WIDTH, FILTER_HIDDEN
    g = pl.BlockSpec((tm, H), lambda i: (i, 0))
    w = pl.BlockSpec((H, C), lambda i: (0, 0))
    vec = pl.BlockSpec((1, C), lambda i: (0, 0))
    out = pl.BlockSpec((tm, C), lambda i: (i, 0))
    return pl.pallas_call(
        functools.partial(_filt_gen_kernel, tm=tm), grid=(SEQ // tm,),
        in_specs=[g, g, w, w, vec, vec, vec, vec], out_specs=[out, out, vec],
        out_shape=[jax.ShapeDtypeStruct((SEQ, C), BF16)] * 2 + [jax.ShapeDtypeStruct((1, C), F32)],
        compiler_params=_params(1), name="filt_gen")(
            gf, gr, w3[:, :C], w3[:, C:], b3[None, :C], b3[None, C:], decay[0:1], decay[1:2])


def _fft_tables():
    N = FFT_N
    n2 = jnp.arange(FFT_R, dtype=jnp.int32)[:, None, None]
    k1 = jnp.arange(FFT_R // 2, dtype=jnp.int32)[None, :, None]
    n1 = jnp.arange(FFT_R, dtype=jnp.int32)[None, None, :]
    idx = ((FFT_R * n1 + n2) * (2 * k1 + 1)) % (2 * N)
    theta = idx.astype(F32) * (math.pi / N)
    c, s = jnp.cos(theta), jnp.sin(theta)
    ta = jnp.concatenate([c, -s], axis=1)
    ti = jnp.transpose(ta[:, :, :FFT_R // 2], (0, 2, 1)) * (2.0 / N)
    j = jnp.arange(FFT_R, dtype=jnp.int32)
    phi = ((j[:, None] * j[None, :]) % FFT_R).astype(F32) * (2.0 * math.pi / FFT_R)
    cm, sm = jnp.cos(phi), jnp.sin(phi)
    f2f = jnp.block([[cm, sm], [-sm, cm]])
    f2i = jnp.block([[cm, -sm], [sm, cm]])
    return ta.astype(BF16), ti.astype(BF16), f2f.astype(BF16), f2i.astype(BF16)


def _fft_conv_kernel(z_ref, kf_ref, kb_ref, ta_ref, ti_ref, f2f_ref, f2i_ref, y_ref, pz_ref, pk_ref, s_ref):
    half = FFT_R // 2
    tc = z_ref.shape[1]
    zeros_hi = jnp.zeros((half, tc), BF16)

    for c in range(FFT_R // N2_CHUNK):
        off = c * N2_CHUNK
        for n1 in range(half):
            pz_ref[pl.ds(n1 * P_PITCH, N2_CHUNK), :] = z_ref[pl.ds(n1 * FFT_R + off, N2_CHUNK), :].astype(F32)
            pk_ref[pl.ds(n1 * P_PITCH, N2_CHUNK), :] = kf_ref[pl.ds(n1 * FFT_R + off, N2_CHUNK), :].astype(F32)
            pk_ref[pl.ds((half + n1) * P_PITCH, N2_CHUNK), :] = (
                kb_ref[pl.ds(n1 * FFT_R + off, N2_CHUNK), :].astype(F32))

        def stage_a(j, carry):
            n2 = off + j
            xz = pz_ref[pl.ds(j, half, stride=P_PITCH), :].astype(BF16)
            xk = pk_ref[pl.ds(j, FFT_R, stride=P_PITCH), :].astype(BF16)
            rhs = jnp.concatenate([jnp.concatenate([xz, zeros_hi], axis=0), xk], axis=1)
            a = _dot(ta_ref[n2], rhs)
            row = pl.multiple_of(n2 * S_PITCH, 8)
            s_ref[0, pl.ds(row, FFT_R), :] = a[:, :tc]
            s_ref[1, pl.ds(row, FFT_R), :] = a[:, tc:]
            return carry

        lax.fori_loop(0, N2_CHUNK, stage_a, 0, unroll=FFT_UNROLL)

    def rows(k1):
        return pl.ds(k1, FFT_R, stride=S_PITCH), pl.ds(half + k1, FFT_R, stride=S_PITCH)

    def spectrum(k1):
        re_rows, im_rows = rows(k1)
        rhs = jnp.concatenate(
            [jnp.concatenate([s_ref[0, re_rows, :], s_ref[1, re_rows, :]], axis=1),
             jnp.concatenate([s_ref[0, im_rows, :], s_ref[1, im_rows, :]], axis=1)], axis=0).astype(BF16)
        return _dot(f2f_ref[...], rhs)

    def product(x):
        zr, zi = x[:FFT_R, :tc], x[FFT_R:, :tc]
        kr, ki = x[:FFT_R, tc:], x[FFT_R:, tc:]
        return jnp.concatenate([zr * kr - zi * ki, zr * ki + zi * kr], axis=0).astype(BF16)

    def stage_b(i, carry):
        k1s = [i * FFT_KB + u for u in range(FFT_KB)]
        xs = [spectrum(k1) for k1 in k1s]
        for u in range(0, FFT_KB, 2):
            d = _dot(f2i_ref[...], jnp.concatenate([product(xs[u]), product(xs[u + 1])], axis=1))
            for v in range(2):
                re_rows, im_rows = rows(k1s[u + v])
                s_ref[0, re_rows, :] = d[:FFT_R, v * tc:(v + 1) * tc]
                s_ref[0, im_rows, :] = d[FFT_R:, v * tc:(v + 1) * tc]
        return carry

    lax.fori_loop(0, half // FFT_KB, stage_b, 0)

    def stage_out(n2, carry):
        d = s_ref[0, pl.ds(pl.multiple_of(n2 * S_PITCH, 8), FFT_R), :].astype(BF16)
        y_ref[pl.ds(n2, half, stride=FFT_R), :] = _dot(ti_ref[n2], d)
        return carry

    lax.fori_loop(0, FFT_R, stage_out, 0, unroll=FFT_UNROLL)


def _fft_conv(z, kf, kb, tables):
    L, C = z.shape
    tc = FFT_TC
    ta, ti, f2f, f2i = tables
    col = pl.BlockSpec((L, tc), lambda j: (0, j))
    full = lambda a: pl.BlockSpec(a.shape, lambda j: (0,) * a.ndim)
    half = FFT_R // 2
    return pl.pallas_call(
        _fft_conv_kernel, grid=(C // tc,),
        in_specs=[col, col, col, full(ta), full(ti), full(f2f), full(f2i)],
        out_specs=col, out_shape=jax.ShapeDtypeStruct((L, C), F32),
        scratch_shapes=[pltpu.VMEM((half * P_PITCH, tc), F32), pltpu.VMEM((FFT_R * P_PITCH, tc), F32),
                        pltpu.VMEM((2, FFT_R * S_PITCH, tc), F32)],
        compiler_params=_params(1), name="fft_conv")(z, kf, kb, ta, ti, f2f, f2i)


def _outproj_kernel(a_ref, x0_ref, z_ref, yc_ref, norm_ref, skip_ref, ag_ref, hg_ref,
                    wa_ref, wh_ref, b_ref, x_ref, g_ref, beta_ref, xf_ref, xb_ref):
    a = _rms_norm(a_ref[...].astype(F32), ag_ref[...])
    z = z_ref[...].astype(F32)
    hy = x0_ref[...].astype(F32) * (yc_ref[...] * (1.0 / norm_ref[...]) + skip_ref[...] * z)
    h = _rms_norm(hy, hg_ref[...])
    mix = _dot(a.astype(BF16), wa_ref[...]) + _dot(h.astype(BF16), wh_ref[...]) + b_ref[...]
    y = _layer_norm(ALPHA * x_ref[...] + mix, g_ref[...], beta_ref[...])
    xf_ref[...] = y
    xb_ref[...] = y.astype(BF16)


def _outproj(a, x0, z, yc, norm, skip, ag, hg, w_out, b_out, x, g, beta, tm=256):
    L, D = x.shape
    C = HYENA_WIDTH
    half = pl.BlockSpec((tm, C), lambda i: (i, 0))
    hvec = pl.BlockSpec((1, C), lambda i: (0, 0))
    row = pl.BlockSpec((tm, D), lambda i: (i, 0))
    vec = pl.BlockSpec((1, D), lambda i: (0, 0))
    wa = pl.BlockSpec((ATT_WIDTH, D), lambda i: (0, 0))
    wh = pl.BlockSpec((C, D), lambda i: (1, 0))
    return pl.pallas_call(
        _outproj_kernel, grid=(L // tm,),
        in_specs=[half, half, half, half, hvec, hvec, hvec, hvec, wa, wh, vec, row, vec, vec],
        out_specs=[row, row],
        out_shape=[jax.ShapeDtypeStruct((L, D), F32), jax.ShapeDtypeStruct((L, D), BF16)],
        compiler_params=_params(1), name="outproj_ln")(
            a, x0, z, yc, norm, skip, ag, hg, w_out, w_out, b_out, x, g, beta)


def _gelu_tanh(x):
    return x * (0.5 * (1.0 + jnp.tanh(math.sqrt(2.0 / math.pi) * (x + 0.044715 * (x * x * x)))))


def _ffn_up_kernel(x_ref, xp_ref, xn_ref, wg_ref, wv_ref, bg_ref, bv_ref, cw_ref, cb_ref, h_ref, xe_ref):
    _fill_extended(xe_ref, x_ref, xp_ref, xn_ref)
    gate = _conv3_extended(_dot(xe_ref[...], wg_ref[...]) + bg_ref[...], cw_ref, cb_ref)
    val = _dot(x_ref[...], wv_ref[...]) + bv_ref[...]
    h_ref[...] = (_gelu_tanh(gate) * val).astype(BF16)


def _ffn_up(xb, w_up_b, b_up_p, conv_w, conv_b, tm=1024, tn=512):
    L, K = xb.shape
    nj = D_FF_PAD // tn
    wg = pl.BlockSpec((K, tn), lambda i, j: (0, j))
    wv = pl.BlockSpec((K, tn), lambda i, j: (0, nj + j))
    bg = pl.BlockSpec((1, tn), lambda i, j: (0, j))
    bv = pl.BlockSpec((1, tn), lambda i, j: (0, nj + j))
    return pl.pallas_call(
        _ffn_up_kernel, grid=(L // tm, nj),
        in_specs=_halo_specs(tm, K, L) + [wg, wv, bg, bv, pl.BlockSpec((3, tn), lambda i, j: (0, j)), bg],
        out_specs=pl.BlockSpec((tm, tn), lambda i, j: (i, j)),
        out_shape=jax.ShapeDtypeStruct((L, D_FF_PAD), BF16),
        scratch_shapes=[pltpu.VMEM((tm + 2 * HALO, K), BF16)],
        compiler_params=_params(2), name="ffn_up")(xb, xb, xb, w_up_b, w_up_b, b_up_p, b_up_p, conv_w, conv_b)


def _ffn_down_kernel(h_ref, w_ref, b_ref, x_ref, g_ref, beta_ref, xf_ref, xb_ref, acc_ref):
    k = pl.program_id(1)
    part = _dot(h_ref[...], w_ref[...])

    @pl.when(k == 0)
    def _():
        acc_ref[...] = part

    @pl.when(k > 0)
    def _():
        acc_ref[...] += part

    @pl.when(k == pl.num_programs(1) - 1)
    def _():
        y = _layer_norm(ALPHA * x_ref[...] + acc_ref[...] + b_ref[...], g_ref[...], beta_ref[...])
        xf_ref[...] = y
        xb_ref[...] = y.astype(BF16)


def _ffn_down(h, w_down, b_down, x, g, beta, tm=512, tk=1408):
    L, D = x.shape
    row = pl.BlockSpec((tm, D), lambda i, k: (i, 0))
    vec = pl.BlockSpec((1, D), lambda i, k: (0, 0))
    return pl.pallas_call(
        _ffn_down_kernel, grid=(L // tm, D_FF_PAD // tk),
        in_specs=[pl.BlockSpec((tm, tk), lambda i, k: (i, k)), pl.BlockSpec((tk, D), lambda i, k: (k, 0)),
                  vec, row, vec, vec],
        out_specs=[row, row],
        out_shape=[jax.ShapeDtypeStruct((L, D), F32), jax.ShapeDtypeStruct((L, D), BF16)],
        scratch_shapes=[pltpu.VMEM((tm, D), F32)],
        compiler_params=_params(2), name="ffn_down_ln")(h, w_down, b_down, x, g, beta)


def _rope_tables(L):
    rows = L // GRID_W
    row_pos = jnp.repeat(jnp.arange(rows, dtype=F32), GRID_W)
    col_pos = jnp.tile(jnp.arange(GRID_W, dtype=F32), rows)
    dims = HEAD_DIM // 2

    def axis_table(pos):
        inv = ROPE_THETA ** (-jnp.arange(0, dims, 2, dtype=F32) / dims)
        ang = pos[:, None] * inv[None, :]
        ang = jnp.concatenate([ang, ang], axis=-1)
        return jnp.cos(ang), jnp.sin(ang)

    cos_r, sin_r = axis_table(row_pos)
    cos_c, sin_c = axis_table(col_pos)
    cos = jnp.concatenate([cos_r, cos_c], axis=-1)
    sin = jnp.concatenate([sin_r, sin_c], axis=-1)
    first = (jnp.arange(HEAD_DIM) % dims) < dims // 2
    return cos, jnp.where(first, -sin, 0.0), jnp.where(first, 0.0, sin)


def _pad_ff(a, axis):
    pad = [(0, 0)] * a.ndim
    pad[axis] = (0, D_FF_PAD - D_FF)
    return jnp.pad(a, pad)


def kernel(x, ln_in_g, ln_in_b, w_in, q_norm_g, k_norm_g, hy_conv_w, hy_conv_b, filt_w1, filt_b1, filt_f1, filt_w2, filt_b2, filt_f2, filt_w3, filt_b3, filt_decay, hy_skip, att_out_g, hy_out_g, w_out, b_out, ln1_g, ln1_b, w_up, b_up, ffn_conv_w, ffn_conv_b, w_down, b_down, ln2_g, ln2_b):
    assert x.shape == (1, SEQ, D_MODEL) and w_in.shape == (DEPTH, D_MODEL, ATT_IN + HY_IN)
    cos, sin_a, sin_b = _rope_tables(SEQ)
    tables = _fft_tables()
    fband = jnp.linspace(1e-4, FILTER_BANDS - 1, FILTER_BANDS, dtype=F32)[None, :]

    xf, xb = _input_ln(x[0], ln_in_g[None], ln_in_b[None])
    for l in range(DEPTH):
        w_in_b = w_in[l].astype(BF16)
        qt, k, vt = _inproj_att(xb, w_in_b, q_norm_g[l][None], k_norm_g[l][None], cos, sin_a, sin_b)
        att = _attention(qt, k, vt)
        z, x0 = _inproj_hy(xb, w_in_b, hy_conv_w[l], hy_conv_b[l][None])
        gf, gr = _filt_feat(fband, filt_w1[l], filt_b1[l], filt_f1[l], filt_w2[l], filt_b2[l], filt_f2[l])
        kf, kb, norm = _filt_gen(gf, gr, filt_w3[l], filt_b3[l], filt_decay[l])
        yc = _fft_conv(z, kf, kb, tables)
        xf, xb = _outproj(att, x0, z, yc, norm, hy_skip[l][None], att_out_g[l][None], hy_out_g[l][None],
                          w_out[l].astype(BF16), b_out[l][None], xf, ln1_g[l][None], ln1_b[l][None])
        w_up_b = _pad_ff(w_up[l].astype(BF16).reshape(D_MODEL, 2, D_FF), 2).reshape(D_MODEL, 2 * D_FF_PAD)
        b_up_p = _pad_ff(b_up[l].reshape(2, D_FF), 1).reshape(1, 2 * D_FF_PAD)
        h = _ffn_up(xb, w_up_b, b_up_p, _pad_ff(ffn_conv_w[l], 1), _pad_ff(ffn_conv_b[l], 0)[None])
        xf, xb = _ffn_down(h, _pad_ff(w_down[l].astype(BF16), 0), b_down[l][None], xf, ln2_g[l][None], ln2_b[l][None])
    return xf[None]
```

```python
import functools
import math

import jax
import jax.numpy as jnp
import numpy as np
from jax import lax
from jax.experimental import pallas as pl
from jax.experimental.pallas import tpu as pltpu

F32 = jnp.float32
BF16 = jnp.bfloat16

D_MODEL = 2048
SEQ = 8192
DEPTH = 2
ATT_WIDTH = 1024
HYENA_WIDTH = 1024
HEAD_DIM = 128
N_KV_HEADS = 2
GQA_GROUP = 4
KV_WIDTH = N_KV_HEADS * HEAD_DIM
ATT_IN = ATT_WIDTH + 2 * KV_WIDTH
HY_IN = 3 * HYENA_WIDTH
FILTER_HIDDEN = 64
FILTER_BANDS = 16
D_FF = 5504
GRID_W = 64
ROPE_THETA = 10000.0
ALPHA = (2.0 * DEPTH) ** 0.25
LN_EPS = 1e-5
RMS_EPS = 1e-6

LANES = 128
BF16_SUBLANES = 16
VMEM_LIMIT = 56 * 1024 * 1024

FFT_R = 128
FFT_N = FFT_R * FFT_R
FFT_TC = 128
S_PITCH = 136
N2_CHUNK = 32
P_PITCH = 40
FFT_UNROLL = 8
FFT_KB = 4

_dot = functools.partial(jnp.dot, preferred_element_type=F32)
_dot_hi = functools.partial(jnp.dot, preferred_element_type=F32, precision=lax.Precision.HIGHEST)


def _params(n_axes):
    return pltpu.CompilerParams(dimension_semantics=("arbitrary",) * n_axes,
                                vmem_limit_bytes=VMEM_LIMIT)


def _layer_norm(v, g, b):
    mu = jnp.mean(v, axis=-1, keepdims=True)
    d = v - mu
    var = jnp.mean(d * d, axis=-1, keepdims=True)
    return d * lax.rsqrt(var + LN_EPS) * g + b


def _rms_norm(v, g):
    ms = jnp.mean(v * v, axis=-1, keepdims=True)
    return v * lax.rsqrt(ms + RMS_EPS) * g


def _ln_kernel(x_ref, g_ref, b_ref, xf_ref, xb_ref):
    y = _layer_norm(x_ref[...], g_ref[...], b_ref[...])
    xf_ref[...] = y
    xb_ref[...] = y.astype(BF16)


def _input_ln(x, g, b, tm=512):
    L, D = x.shape
    row = pl.BlockSpec((tm, D), lambda i: (i, 0))
    vec = pl.BlockSpec((1, D), lambda i: (0, 0))
    return pl.pallas_call(
        _ln_kernel, grid=(L // tm,), in_specs=[row, vec, vec], out_specs=[row, row],
        out_shape=[jax.ShapeDtypeStruct((L, D), F32), jax.ShapeDtypeStruct((L, D), BF16)],
        compiler_params=_params(1), name="input_ln")(x, g, b)


def _inproj_head_kernel(x_ref, w_ref, g_ref, cos_ref, sa_ref, sb_ref, o_ref, *, scale, transpose_out):
    acc = _dot(x_ref[...], w_ref[...])
    for h in range(2):
        sl = slice(h * HEAD_DIM, (h + 1) * HEAD_DIM)
        hn = _rms_norm(acc[:, sl], g_ref[...])
        rot = pltpu.roll(hn, HEAD_DIM - 32, 1) * sa_ref[...] + pltpu.roll(hn, 32, 1) * sb_ref[...]
        r = (hn * cos_ref[...] + rot) * scale
        if transpose_out:
            o_ref[sl, :] = r.T.astype(BF16)
        else:
            o_ref[:, sl] = r.astype(BF16)


def _inproj_vt_kernel(x_ref, w_ref, o_ref):
    o_ref[...] = _dot(x_ref[...], w_ref[...]).T.astype(BF16)


def _inproj_att(xb, w_in_b, l, qg, kg, cos, sin_a, sin_b, tm=1024):
    L, K = xb.shape
    tn = 2 * HEAD_DIM
    n_q = ATT_WIDTH // tn
    x_spec = pl.BlockSpec((tm, K), lambda i, j: (i, 0))
    head = pl.BlockSpec((1, HEAD_DIM), lambda i, j: (0, 0))
    tab = pl.BlockSpec((tm, HEAD_DIM), lambda i, j: (i, 0))
    q_scale = HEAD_DIM ** -0.5 * math.log2(math.e)
    qt = pl.pallas_call(
        functools.partial(_inproj_head_kernel, scale=q_scale, transpose_out=True), grid=(L // tm, n_q),
        in_specs=[x_spec, pl.BlockSpec((None, K, tn), lambda i, j: (l, 0, j)), head, tab, tab, tab],
        out_specs=pl.BlockSpec((tn, tm), lambda i, j: (j, i)),
        out_shape=jax.ShapeDtypeStruct((ATT_WIDTH, L), BF16),
        compiler_params=_params(2), name="inproj_q")(xb, w_in_b, qg, cos, sin_a, sin_b)
    k = pl.pallas_call(
        functools.partial(_inproj_head_kernel, scale=1.0, transpose_out=False), grid=(L // tm, 1),
        in_specs=[x_spec, pl.BlockSpec((None, K, tn), lambda i, j: (l, 0, n_q)), head, tab, tab, tab],
        out_specs=pl.BlockSpec((tm, tn), lambda i, j: (i, 0)),
        out_shape=jax.ShapeDtypeStruct((L, KV_WIDTH), BF16),
        compiler_params=_params(2), name="inproj_k")(xb, w_in_b, kg, cos, sin_a, sin_b)
    vt = pl.pallas_call(
        _inproj_vt_kernel, grid=(L // tm, 1),
        in_specs=[x_spec, pl.BlockSpec((None, K, tn), lambda i, j: (l, 0, n_q + 1))],
        out_specs=pl.BlockSpec((tn, tm), lambda i, j: (0, i)),
        out_shape=jax.ShapeDtypeStruct((KV_WIDTH, L), BF16),
        compiler_params=_params(2), name="inproj_v")(xb, w_in_b)
    return qt, k, vt


def _attn_kernel(qt_ref, k_ref, vt_ref, o_ref, m_ref, l_ref, acc_ref):
    kb = pl.program_id(2)

    @pl.when(kb == 0)
    def _():
        m_ref[...] = jnp.full(m_ref.shape, -jnp.inf, F32)
        l_ref[...] = jnp.zeros(l_ref.shape, F32)
        acc_ref[...] = jnp.zeros(acc_ref.shape, F32)

    k = k_ref[...]
    vt = vt_ref[...]
    scores = [_dot(k, qt_ref[h * HEAD_DIM:(h + 1) * HEAD_DIM, :]) for h in range(GQA_GROUP)]
    for h in range(GQA_GROUP):
        s = scores[h]
        m_prev = m_ref[h]
        m_new = jnp.maximum(m_prev, jnp.max(s, axis=0, keepdims=True))
        alpha = jnp.exp2(m_prev - m_new)
        p = jnp.exp2(s - m_new)
        l_ref[h] = alpha * l_ref[h] + jnp.sum(p, axis=0, keepdims=True)
        acc_ref[h] = alpha * acc_ref[h] + _dot(vt, p.astype(BF16))
        m_ref[h] = m_new

    @pl.when(kb == pl.num_programs(2) - 1)
    def _():
        for h in range(GQA_GROUP):
            o_ref[:, h * HEAD_DIM:(h + 1) * HEAD_DIM] = (acc_ref[h] / l_ref[h]).T.astype(BF16)


def _attention(qt, k, vt, tq=256, tk=1024):
    L = k.shape[0]
    gw = GQA_GROUP * HEAD_DIM
    return pl.pallas_call(
        _attn_kernel, grid=(N_KV_HEADS, L // tq, L // tk),
        in_specs=[pl.BlockSpec((gw, tq), lambda g, i, kb: (g, i)),
                  pl.BlockSpec((tk, HEAD_DIM), lambda g, i, kb: (kb, g)),
                  pl.BlockSpec((HEAD_DIM, tk), lambda g, i, kb: (g, kb))],
        out_specs=pl.BlockSpec((tq, gw), lambda g, i, kb: (i, g)),
        out_shape=jax.ShapeDtypeStruct((L, ATT_WIDTH), BF16),
        scratch_shapes=[pltpu.VMEM((GQA_GROUP, 1, tq), F32), pltpu.VMEM((GQA_GROUP, 1, tq), F32),
                        pltpu.VMEM((GQA_GROUP, HEAD_DIM, tq), F32)],
        compiler_params=_params(3), name="attention")(qt, k, vt)


HALO = BF16_SUBLANES


def _halo_specs(tm, K, L):
    hb = tm // HALO
    n_h = L // HALO
    return [pl.BlockSpec((tm, K), lambda i, j: (i, 0)),
            pl.BlockSpec((HALO, K), lambda i, j: (jnp.maximum(i * hb - 1, 0), 0)),
            pl.BlockSpec((HALO, K), lambda i, j: (jnp.minimum((i + 1) * hb, n_h - 1), 0))]


def _fill_extended(xe_ref, x_ref, xp_ref, xn_ref):
    tm = x_ref.shape[0]

    @pl.when(pl.program_id(1) == 0)
    def _():
        xe_ref[0:HALO, :] = xp_ref[...]
        xe_ref[HALO:HALO + tm, :] = x_ref[...]
        xe_ref[HALO + tm:, :] = xn_ref[...]


def _conv3_extended(u, w_ref, b_ref):
    n_ext = u.shape[0]
    tm = n_ext - 2 * HALO
    i = pl.program_id(0)
    top = u[:HALO] * jnp.where(i > 0, 1.0, 0.0)
    bot = u[HALO + tm:] * jnp.where(i < pl.num_programs(0) - 1, 1.0, 0.0)
    u = jnp.concatenate([top, u[HALO:HALO + tm], bot], axis=0)
    um = pltpu.roll(u, 1, 0)[HALO:HALO + tm]
    up = pltpu.roll(u, n_ext - 1, 0)[HALO:HALO + tm]
    return w_ref[0:1, :] * um + w_ref[1:2, :] * u[HALO:HALO + tm] + w_ref[2:3, :] * up + b_ref[...]


def _inproj_hy_kernel(x_ref, xp_ref, xn_ref, w0_ref, w1_ref, w2_ref, cw0_ref, cw1_ref, cw2_ref,
                      cb0_ref, cb1_ref, cb2_ref, z_ref, x0_ref, xe_ref):
    _fill_extended(xe_ref, x_ref, xp_ref, xn_ref)
    xe = xe_ref[...]
    x0_ref[...] = _conv3_extended(_dot(xe, w0_ref[...]), cw0_ref, cb0_ref).astype(BF16)
    x1 = _conv3_extended(_dot(xe, w1_ref[...]), cw1_ref, cb1_ref)
    v = _conv3_extended(_dot(xe, w2_ref[...]), cw2_ref, cb2_ref)
    z_ref[...] = (x1 * v).astype(BF16)


def _inproj_hy(xb, w_in_b, l, conv_w, conv_b, tm=1024, tn=256):
    L, K = xb.shape
    C = HYENA_WIDTH
    nj = C // tn
    w_specs = [pl.BlockSpec((None, K, tn), functools.partial(lambda i, j, g: (l, 0, (ATT_IN + g * C) // tn + j), g=g))
               for g in range(3)]
    cw_specs = [pl.BlockSpec((3, tn), functools.partial(lambda i, j, g: (0, g * nj + j), g=g)) for g in range(3)]
    cb_specs = [pl.BlockSpec((1, tn), functools.partial(lambda i, j, g: (0, g * nj + j), g=g)) for g in range(3)]
    out = pl.BlockSpec((tm, tn), lambda i, j: (i, j))
    return pl.pallas_call(
        _inproj_hy_kernel, grid=(L // tm, nj),
        in_specs=_halo_specs(tm, K, L) + w_specs + cw_specs + cb_specs,
        out_specs=[out, out], out_shape=[jax.ShapeDtypeStruct((L, C), BF16)] * 2,
        scratch_shapes=[pltpu.VMEM((tm + 2 * HALO, K), BF16)],
        compiler_params=_params(2), name="inproj_hy")(
            xb, xb, xb, w_in_b, w_in_b, w_in_b, conv_w, conv_w, conv_w, conv_b, conv_b, conv_b)


def _filt_feat_kernel(fb_ref, w1t_ref, w1c_ref, w1s_ref, b1_ref, f1_ref, w2_ref, b2_ref, f2_ref,
                      gf_ref, gr_ref, *, tm):
    pos = lax.broadcasted_iota(jnp.int32, (1, tm), 1) + pl.program_id(0) * tm

    def feats(p):
        pf = p.astype(F32)
        ang = fb_ref[...] * ((2.0 * math.pi * pf) / SEQ)
        pre = (w1t_ref[...] * (pf / (SEQ - 1)) + _dot_hi(w1c_ref[...], jnp.cos(ang))
               + _dot_hi(w1s_ref[...], -jnp.sin(ang)) + b1_ref[...])
        h = jnp.sin(f1_ref[...] * pre)
        return jnp.sin(f2_ref[...] * (_dot_hi(w2_ref[...], h) + b2_ref[...]))

    gf_ref[...] = feats(pos)
    gr_ref[...] = feats(jnp.where(pos == 0, 0, SEQ - pos))


def _filt_feat(fband, w1, b1, f1, w2, b2, f2, tm=1024):
    H = FILTER_HIDDEN
    full = lambda a: pl.BlockSpec(a.shape, lambda i: (0,) * a.ndim)
    col = lambda v: v[:, None]
    args = [col(fband), col(w1[0]), w1[1:1 + FILTER_BANDS].T, w1[1 + FILTER_BANDS:].T, col(b1), col(f1),
            w2.T, col(b2), col(f2)]
    out = pl.BlockSpec((H, tm), lambda i: (0, i))
    return pl.pallas_call(
        functools.partial(_filt_feat_kernel, tm=tm), grid=(SEQ // tm,),
        in_specs=[full(a) for a in args], out_specs=[out, out],
        out_shape=[jax.ShapeDtypeStruct((H, SEQ), F32)] * 2,
        compiler_params=_params(1), name="filt_feat")(*args)


def _dot_t_hi(a_t, b):
    return lax.dot_general(a_t, b, (((0,), (0,)), ((), ())), preferred_element_type=F32,
                           precision=lax.Precision.HIGHEST)


def _filt_gen_kernel(gf_ref, gr_ref, w3f_ref, w3b_ref, b3f_ref, b3b_ref, df_ref, db_ref,
                     kf_ref, kb_ref, norm_ref, *, tm):
    i = pl.program_id(0)
    pos = lax.broadcasted_iota(jnp.int32, (tm, HYENA_WIDTH), 0) + i * tm
    pos_r = jnp.where(pos == 0, 0, SEQ - pos)
    t = pos.astype(F32) / (SEQ - 1)
    tr = pos_r.astype(F32) / (SEQ - 1)
    hf = (_dot_t_hi(gf_ref[...], w3f_ref[...]) + b3f_ref[...]) * jnp.exp(-t * jnp.abs(df_ref[...]))
    hb = (_dot_t_hi(gr_ref[...], w3b_ref[...]) + b3b_ref[...]) * jnp.exp(-tr * jnp.abs(db_ref[...]))

    @pl.when(i == 0)
    def _():
        norm_ref[...] = jnp.zeros(norm_ref.shape, F32)

    norm_ref[...] += jnp.sum(jnp.abs(hf) + jnp.abs(hb), axis=0, keepdims=True)
    kf_ref[...] = hf.astype(BF16)
    kb_ref[...] = jnp.where(pos == 0, 0.0, -hb).astype(BF16)


def _filt_gen(gf, gr, w3, b3, decay, tm=512):
    C, H = HYENA_WIDTH, FILTER_HIDDEN
    g = pl.BlockSpec((H, tm), lambda i: (0, i))
    w = pl.BlockSpec((H, C), lambda i: (0, 0))
    vec = pl.BlockSpec((1, C), lambda i: (0, 0))
    out = pl.BlockSpec((tm, C), lambda i: (i, 0))
    return pl.pallas_call(
        functools.partial(_filt_gen_kernel, tm=tm), grid=(SEQ // tm,),
        in_specs=[g, g, w, w, vec, vec, vec, vec], out_specs=[out, out, vec],
        out_shape=[jax.ShapeDtypeStruct((SEQ, C), BF16)] * 2 + [jax.ShapeDtypeStruct((1, C), F32)],
        compiler_params=_params(1), name="filt_gen")(
            gf, gr, w3[:, :C], w3[:, C:], b3[None, :C], b3[None, C:], decay[0:1], decay[1:2])


def _fft_tables():
    N = FFT_N
    n2 = np.arange(FFT_R, dtype=np.int64)[:, None, None]
    k1 = np.arange(FFT_R // 2, dtype=np.int64)[None, :, None]
    n1 = np.arange(FFT_R, dtype=np.int64)[None, None, :]
    theta = (((FFT_R * n1 + n2) * (2 * k1 + 1)) % (2 * N)) * (math.pi / N)
    c, s = np.cos(theta), np.sin(theta)
    ta = np.concatenate([c, -s], axis=1)
    ti = np.transpose(ta[:, :, :FFT_R // 2], (0, 2, 1)) * (2.0 / N)
    j = np.arange(FFT_R, dtype=np.int64)
    phi = ((j[:, None] * j[None, :]) % FFT_R) * (2.0 * math.pi / FFT_R)
    cm, sm = np.cos(phi), np.sin(phi)
    f2f = np.block([[cm, sm], [-sm, cm]])
    f2i = np.block([[cm, -sm], [sm, cm]])
    return tuple(jnp.asarray(t.astype(np.float32)).astype(BF16) for t in (ta, ti, f2f, f2i))


def _fft_conv_kernel(z_ref, kf_ref, kb_ref, ta_ref, ti_ref, f2f_ref, f2i_ref, y_ref, pz_ref, pk_ref, s_ref):
    half = FFT_R // 2
    tc = z_ref.shape[1]
    zeros_hi = jnp.zeros((half, tc), BF16)

    for c in range(FFT_R // N2_CHUNK):
        off = c * N2_CHUNK
        for n1 in range(half):
            pz_ref[pl.ds(n1 * P_PITCH, N2_CHUNK), :] = z_ref[pl.ds(n1 * FFT_R + off, N2_CHUNK), :].astype(F32)
            pk_ref[pl.ds(n1 * P_PITCH, N2_CHUNK), :] = kf_ref[pl.ds(n1 * FFT_R + off, N2_CHUNK), :].astype(F32)
            pk_ref[pl.ds((half + n1) * P_PITCH, N2_CHUNK), :] = (
                kb_ref[pl.ds(n1 * FFT_R + off, N2_CHUNK), :].astype(F32))

        def stage_a(j, carry):
            n2 = off + j
            xz = pz_ref[pl.ds(j, half, stride=P_PITCH), :].astype(BF16)
            xk = pk_ref[pl.ds(j, FFT_R, stride=P_PITCH), :].astype(BF16)
            rhs = jnp.concatenate([jnp.concatenate([xz, zeros_hi], axis=0), xk], axis=1)
            a = _dot(ta_ref[n2], rhs)
            row = pl.multiple_of(n2 * S_PITCH, 8)
            s_ref[0, pl.ds(row, FFT_R), :] = a[:, :tc]
            s_ref[1, pl.ds(row, FFT_R), :] = a[:, tc:]
            return carry

        lax.fori_loop(0, N2_CHUNK, stage_a, 0, unroll=FFT_UNROLL)

    def rows(k1):
        return pl.ds(k1, FFT_R, stride=S_PITCH), pl.ds(half + k1, FFT_R, stride=S_PITCH)

    def spectrum(k1):
        re_rows, im_rows = rows(k1)
        rhs = jnp.concatenate(
            [jnp.concatenate([s_ref[0, re_rows, :], s_ref[1, re_rows, :]], axis=1),
             jnp.concatenate([s_ref[0, im_rows, :], s_ref[1, im_rows, :]], axis=1)], axis=0).astype(BF16)
        return _dot(f2f_ref[...], rhs)

    def product(x):
        zr, zi = x[:FFT_R, :tc], x[FFT_R:, :tc]
        kr, ki = x[:FFT_R, tc:], x[FFT_R:, tc:]
        return jnp.concatenate([zr * kr - zi * ki, zr * ki + zi * kr], axis=0).astype(BF16)

    def stage_b(i, carry):
        k1s = [i * FFT_KB + u for u in range(FFT_KB)]
        xs = [spectrum(k1) for k1 in k1s]
        for u in range(0, FFT_KB, 2):
            d = _dot(f2i_ref[...], jnp.concatenate([product(xs[u]), product(xs[u + 1])], axis=1))
            for v in range(2):
                re_rows, im_rows = rows(k1s[u + v])
                s_ref[0, re_rows, :] = d[:FFT_R, v * tc:(v + 1) * tc]
                s_ref[0, im_rows, :] = d[FFT_R:, v * tc:(v + 1) * tc]
        return carry

    lax.fori_loop(0, half // FFT_KB, stage_b, 0)

    def stage_out(n2, carry):
        d = s_ref[0, pl.ds(pl.multiple_of(n2 * S_PITCH, 8), FFT_R), :].astype(BF16)
        y_ref[pl.ds(n2, half, stride=FFT_R), :] = _dot(ti_ref[n2], d)
        return carry

    lax.fori_loop(0, FFT_R, stage_out, 0, unroll=FFT_UNROLL)


def _fft_conv(z, kf, kb, tables):
    L, C = z.shape
    tc = FFT_TC
    ta, ti, f2f, f2i = tables
    col = pl.BlockSpec((L, tc), lambda j: (0, j))
    full = lambda a: pl.BlockSpec(a.shape, lambda j: (0,) * a.ndim)
    half = FFT_R // 2
    return pl.pallas_call(
        _fft_conv_kernel, grid=(C // tc,),
        in_specs=[col, col, col, full(ta), full(ti), full(f2f), full(f2i)],
        out_specs=col, out_shape=jax.ShapeDtypeStruct((L, C), F32),
        scratch_shapes=[pltpu.VMEM((half * P_PITCH, tc), F32), pltpu.VMEM((FFT_R * P_PITCH, tc), F32),
                        pltpu.VMEM((2, FFT_R * S_PITCH, tc), F32)],
        compiler_params=_params(1), name="fft_conv")(z, kf, kb, ta, ti, f2f, f2i)


def _outproj_kernel(a_ref, x0_ref, z_ref, yc_ref, norm_ref, skip_ref, ag_ref, hg_ref,
                    wa_ref, wh_ref, b_ref, x_ref, g_ref, beta_ref, xf_ref, xb_ref):
    a = _rms_norm(a_ref[...].astype(F32), ag_ref[...])
    z = z_ref[...].astype(F32)
    hy = x0_ref[...].astype(F32) * (yc_ref[...] * (1.0 / norm_ref[...]) + skip_ref[...] * z)
    h = _rms_norm(hy, hg_ref[...])
    mix = _dot(a.astype(BF16), wa_ref[...]) + _dot(h.astype(BF16), wh_ref[...]) + b_ref[...]
    y = _layer_norm(ALPHA * x_ref[...] + mix, g_ref[...], beta_ref[...])
    xf_ref[...] = y
    xb_ref[...] = y.astype(BF16)


def _outproj(a, x0, z, yc, norm, skip, ag, hg, w_out, l, b_out, x, g, beta, tm=256):
    L, D = x.shape
    C = HYENA_WIDTH
    half = pl.BlockSpec((tm, C), lambda i: (i, 0))
    hvec = pl.BlockSpec((1, C), lambda i: (0, 0))
    row = pl.BlockSpec((tm, D), lambda i: (i, 0))
    vec = pl.BlockSpec((1, D), lambda i: (0, 0))
    wa = pl.BlockSpec((None, ATT_WIDTH, D), lambda i: (l, 0, 0))
    wh = pl.BlockSpec((None, C, D), lambda i: (l, 1, 0))
    return pl.pallas_call(
        _outproj_kernel, grid=(L // tm,),
        in_specs=[half, half, half, half, hvec, hvec, hvec, hvec, wa, wh, vec, row, vec, vec],
        out_specs=[row, row],
        out_shape=[jax.ShapeDtypeStruct((L, D), F32), jax.ShapeDtypeStruct((L, D), BF16)],
        compiler_params=_params(1), name="outproj_ln")(
            a, x0, z, yc, norm, skip, ag, hg, w_out, w_out, b_out, x, g, beta)


def _gelu_tanh(x):
    return x * (0.5 * (1.0 + jnp.tanh(math.sqrt(2.0 / math.pi) * (x + 0.044715 * (x * x * x)))))


def _ffn_up_kernel(x_ref, xp_ref, xn_ref, wg_ref, wv_ref, bg_ref, bv_ref, cw_ref, cb_ref, h_ref, xe_ref):
    _fill_extended(xe_ref, x_ref, xp_ref, xn_ref)
    gate = _conv3_extended(_dot(xe_ref[...], wg_ref[...]) + bg_ref[...], cw_ref, cb_ref)
    val = _dot(x_ref[...], wv_ref[...]) + bv_ref[...]
    h_ref[...] = (_gelu_tanh(gate) * val).astype(BF16)


def _ffn_up(xb, w_gate, w_val, b_gate, b_val, conv_w, conv_b, l, tm=1024, tn=512):
    L, K = xb.shape
    w = pl.BlockSpec((None, K, tn), lambda i, j: (l, 0, j))
    vec = pl.BlockSpec((1, tn), lambda i, j: (0, j))
    return pl.pallas_call(
        _ffn_up_kernel, grid=(L // tm, pl.cdiv(D_FF, tn)),
        in_specs=_halo_specs(tm, K, L) + [w, w, vec, vec, pl.BlockSpec((3, tn), lambda i, j: (0, j)), vec],
        out_specs=pl.BlockSpec((tm, tn), lambda i, j: (i, j)),
        out_shape=jax.ShapeDtypeStruct((L, D_FF), BF16),
        scratch_shapes=[pltpu.VMEM((tm + 2 * HALO, K), BF16)],
        compiler_params=_params(2), name="ffn_up")(xb, xb, xb, w_gate, w_val, b_gate, b_val, conv_w, conv_b)


def _ffn_down_kernel(h_ref, w_ref, b_ref, x_ref, g_ref, beta_ref, xf_ref, xb_ref):
    ffn = _dot(h_ref[...], w_ref[...]) + b_ref[...]
    y = _layer_norm(ALPHA * x_ref[...] + ffn, g_ref[...], beta_ref[...])
    xf_ref[...] = y
    xb_ref[...] = y.astype(BF16)


def _ffn_down(h, w_down, b_down, x, g, beta, l, tm=256):
    L, D = x.shape
    row = pl.BlockSpec((tm, D), lambda i: (i, 0))
    vec = pl.BlockSpec((1, D), lambda i: (0, 0))
    return pl.pallas_call(
        _ffn_down_kernel, grid=(L // tm,),
        in_specs=[pl.BlockSpec((tm, D_FF), lambda i: (i, 0)),
                  pl.BlockSpec((None, D_FF, D), lambda i: (l, 0, 0), pipeline_mode=pl.Buffered(1)),
                  vec, row, vec, vec],
        out_specs=[row, row],
        out_shape=[jax.ShapeDtypeStruct((L, D), F32), jax.ShapeDtypeStruct((L, D), BF16)],
        compiler_params=_params(1), name="ffn_down_ln")(h, w_down, b_down, x, g, beta)


def _rope_tables(L):
    rows = L // GRID_W
    row_pos = np.repeat(np.arange(rows, dtype=np.float64), GRID_W)
    col_pos = np.tile(np.arange(GRID_W, dtype=np.float64), rows)
    dims = HEAD_DIM // 2

    def axis_table(pos):
        inv = ROPE_THETA ** (-np.arange(0, dims, 2, dtype=np.float64) / dims)
        ang = pos[:, None] * inv[None, :]
        ang = np.concatenate([ang, ang], axis=-1)
        return np.cos(ang), np.sin(ang)

    cos_r, sin_r = axis_table(row_pos)
    cos_c, sin_c = axis_table(col_pos)
    cos = np.concatenate([cos_r, cos_c], axis=-1)
    sin = np.concatenate([sin_r, sin_c], axis=-1)
    first = (np.arange(HEAD_DIM) % dims) < dims // 2
    return tuple(jnp.asarray(t.astype(np.float32)) for t in (cos, np.where(first, -sin, 0.0), np.where(first, 0.0, sin)))


def kernel(x, ln_in_g, ln_in_b, w_in, q_norm_g, k_norm_g, hy_conv_w, hy_conv_b, filt_w1, filt_b1, filt_f1, filt_w2, filt_b2, filt_f2, filt_w3, filt_b3, filt_decay, hy_skip, att_out_g, hy_out_g, w_out, b_out, ln1_g, ln1_b, w_up, b_up, ffn_conv_w, ffn_conv_b, w_down, b_down, ln2_g, ln2_b):
    assert x.shape == (1, SEQ, D_MODEL) and w_in.shape == (DEPTH, D_MODEL, ATT_IN + HY_IN)
    cos, sin_a, sin_b = _rope_tables(SEQ)
    tables = _fft_tables()
    fband = jnp.linspace(1e-4, FILTER_BANDS - 1, FILTER_BANDS, dtype=F32)

    w_in_b = w_in.astype(BF16)
    w_out_b = w_out.astype(BF16)
    w_gate_b = w_up[:, :, :D_FF].astype(BF16)
    w_val_b = w_up[:, :, D_FF:].astype(BF16)
    w_down_b = w_down.astype(BF16)

    xf, xb = _input_ln(x[0], ln_in_g[None], ln_in_b[None])
    for l in range(DEPTH):
        qt, k, vt = _inproj_att(xb, w_in_b, l, q_norm_g[l][None], k_norm_g[l][None], cos, sin_a, sin_b)
        att = _attention(qt, k, vt)
        z, x0 = _inproj_hy(xb, w_in_b, l, hy_conv_w[l], hy_conv_b[l][None])
        gf, gr = _filt_feat(fband, filt_w1[l], filt_b1[l], filt_f1[l], filt_w2[l], filt_b2[l], filt_f2[l])
        kf, kb, norm = _filt_gen(gf, gr, filt_w3[l], filt_b3[l], filt_decay[l])
        yc = _fft_conv(z, kf, kb, tables)
        xf, xb = _outproj(att, x0, z, yc, norm, hy_skip[l][None], att_out_g[l][None], hy_out_g[l][None],
                          w_out_b, l, b_out[l][None], xf, ln1_g[l][None], ln1_b[l][None])
        h = _ffn_up(xb, w_gate_b, w_val_b, b_up[l][None, :D_FF], b_up[l][None, D_FF:],
                    ffn_conv_w[l], ffn_conv_b[l][None], l)
        xf, xb = _ffn_down(h, w_down_b, b_down[l][None], xf, ln2_g[l][None], ln2_b[l][None], l)
    return xf[None]
```

```python
import functools
import math

import jax
import jax.numpy as jnp
import numpy as np
from jax import lax
from jax.experimental import pallas as pl
from jax.experimental.pallas import tpu as pltpu

F32 = jnp.float32
BF16 = jnp.bfloat16

D_MODEL = 2048
SEQ = 8192
DEPTH = 2
ATT_WIDTH = 1024
HYENA_WIDTH = 1024
HEAD_DIM = 128
N_KV_HEADS = 2
GQA_GROUP = 4
KV_WIDTH = N_KV_HEADS * HEAD_DIM
ATT_IN = ATT_WIDTH + 2 * KV_WIDTH
HY_IN = 3 * HYENA_WIDTH
FILTER_HIDDEN = 64
FILTER_BANDS = 16
D_FF = 5504
GRID_W = 64
ROPE_THETA = 10000.0
ALPHA = (2.0 * DEPTH) ** 0.25
LN_EPS = 1e-5
RMS_EPS = 1e-6

LANES = 128
BF16_SUBLANES = 16
VMEM_LIMIT = 56 * 1024 * 1024
ATT_CHUNK = 128

FFT_R = 128
FFT_N = FFT_R * FFT_R
FFT_TC = 128
S_PITCH = 136
N2_CHUNK = 32
P_PITCH = 40
FFT_UNROLL = 8
FFT_KB = 4

_dot = functools.partial(jnp.dot, preferred_element_type=F32)
_dot_hi = functools.partial(jnp.dot, preferred_element_type=F32, precision=lax.Precision.HIGHEST)


def _params(n_axes, flags=None):
    return pltpu.CompilerParams(dimension_semantics=("arbitrary",) * n_axes,
                                vmem_limit_bytes=VMEM_LIMIT, flags=flags)


def _layer_norm(v, g, b):
    mu = jnp.mean(v, axis=-1, keepdims=True)
    d = v - mu
    var = jnp.mean(d * d, axis=-1, keepdims=True)
    return d * lax.rsqrt(var + LN_EPS) * g + b


def _rms_norm(v, g):
    ms = jnp.mean(v * v, axis=-1, keepdims=True)
    return v * lax.rsqrt(ms + RMS_EPS) * g


def _ln_kernel(x_ref, g_ref, b_ref, xf_ref, xb_ref):
    y = _layer_norm(x_ref[...], g_ref[...], b_ref[...])
    xf_ref[...] = y
    xb_ref[...] = y.astype(BF16)


def _input_ln(x, g, b, tm=512):
    L, D = x.shape
    row = pl.BlockSpec((tm, D), lambda i: (i, 0))
    vec = pl.BlockSpec((1, D), lambda i: (0, 0))
    return pl.pallas_call(
        _ln_kernel, grid=(L // tm,), in_specs=[row, vec, vec], out_specs=[row, row],
        out_shape=[jax.ShapeDtypeStruct((L, D), F32), jax.ShapeDtypeStruct((L, D), BF16)],
        compiler_params=_params(1), name="input_ln")(x, g, b)


def _inproj_k_kernel(x_ref, w_ref, g_ref, cos_ref, sa_ref, sb_ref, o_ref):
    acc = _dot(x_ref[...], w_ref[...])
    for h in range(w_ref.shape[1] // HEAD_DIM):
        sl = slice(h * HEAD_DIM, (h + 1) * HEAD_DIM)
        hn = _rms_norm(acc[:, sl], g_ref[...])
        rot = pltpu.roll(hn, HEAD_DIM - 32, 1) * sa_ref[...] + pltpu.roll(hn, 32, 1) * sb_ref[...]
        o_ref[:, sl] = (hn * cos_ref[...] + rot).astype(BF16)


def _inproj_qt_kernel(x_ref, w_ref, g_ref, cos_ref, sa_ref, sb_ref, o_ref, *, scale):
    acc = _dot(x_ref[...], w_ref[...])
    tm = acc.shape[0]
    gain = pltpu.repeat(g_ref[...], tm // LANES, axis=1)
    for h in range(w_ref.shape[1] // HEAD_DIM):
        sl = slice(h * HEAD_DIM, (h + 1) * HEAD_DIM)
        xt = acc[:, sl].T
        hn = xt * lax.rsqrt(jnp.mean(xt * xt, axis=0, keepdims=True) + RMS_EPS) * gain
        rot = pltpu.roll(hn, HEAD_DIM - 32, 0) * sa_ref[...] + pltpu.roll(hn, 32, 0) * sb_ref[...]
        o_ref[sl, :] = ((hn * cos_ref[...] + rot) * scale).astype(BF16)


def _inproj_vt_kernel(x_ref, w_ref, o_ref):
    o_ref[...] = _dot(x_ref[...], w_ref[...]).T.astype(BF16)


def _inproj_att(xb, w_in_b, l, qg, kg, tabs, tabs_t, tm=1024):
    cos, sin_a, sin_b = tabs
    L, K = xb.shape
    tn = KV_WIDTH
    tq = 2 * tn
    n_q = ATT_WIDTH // tn
    x_spec = pl.BlockSpec((tm, K), lambda i, j: (i, 0))
    head = pl.BlockSpec((1, HEAD_DIM), lambda i, j: (0, 0))
    tab = pl.BlockSpec((tm, HEAD_DIM), lambda i, j: (i, 0))
    q_scale = HEAD_DIM ** -0.5 * math.log2(math.e)
    tab_t = pl.BlockSpec((HEAD_DIM, tm), lambda i, j: (0, i))
    qt = pl.pallas_call(
        functools.partial(_inproj_qt_kernel, scale=q_scale), grid=(L // tm, ATT_WIDTH // tq),
        in_specs=[x_spec, pl.BlockSpec((None, K, tq), lambda i, j: (l, 0, j)),
                  pl.BlockSpec((HEAD_DIM, LANES), lambda i, j: (0, 0)), tab_t, tab_t, tab_t],
        out_specs=pl.BlockSpec((tq, tm), lambda i, j: (j, i)),
        out_shape=jax.ShapeDtypeStruct((ATT_WIDTH, L), BF16),
        compiler_params=_params(2), name="inproj_q")(
            xb, w_in_b, jnp.broadcast_to(qg.reshape(HEAD_DIM, 1), (HEAD_DIM, LANES)), *tabs_t)
    k = pl.pallas_call(
        _inproj_k_kernel, grid=(L // tm, 1),
        in_specs=[x_spec, pl.BlockSpec((None, K, tn), lambda i, j: (l, 0, n_q)), head, tab, tab, tab],
        out_specs=pl.BlockSpec((tm, tn), lambda i, j: (i, 0)),
        out_shape=jax.ShapeDtypeStruct((L, KV_WIDTH), BF16),
        compiler_params=_params(2), name="inproj_k")(xb, w_in_b, kg, cos, sin_a, sin_b)
    vt = pl.pallas_call(
        _inproj_vt_kernel, grid=(L // tm, 1),
        in_specs=[x_spec, pl.BlockSpec((None, K, tn), lambda i, j: (l, 0, n_q + 1))],
        out_specs=pl.BlockSpec((tn, tm), lambda i, j: (0, i)),
        out_shape=jax.ShapeDtypeStruct((KV_WIDTH, L), BF16),
        compiler_params=_params(2), name="inproj_v")(xb, w_in_b)
    return qt, k, vt


def _attn_kernel(qt_ref, k0_ref, ka_ref, kb_ref, vt_ref, o_ref, m_ref, l_ref, acc_ref,
                 s0_ref, s1_ref, pa_ref, pb_ref):
    j = pl.program_id(2)
    tk, tq = s0_ref.shape[1:]
    sub = 8
    chunks = [pl.ds(c * ATT_CHUNK, ATT_CHUNK) for c in range(tk // ATT_CHUNK)]
    heads = range(GQA_GROUP)

    def scores(k_ref, s_ref):
        k = k_ref[...]
        for h in heads:
            s_ref[h] = _dot(k, qt_ref[h * HEAD_DIM:(h + 1) * HEAD_DIM, :])

    def softmax_pv(s_ref, p_ref, vt):
        for h in heads:
            m_part = jnp.full((sub, tq), -jnp.inf, F32)
            for rows in chunks:
                m_part = jnp.maximum(m_part, jnp.max(s_ref[h, rows, :].reshape(ATT_CHUNK // sub, sub, tq), axis=0))
            m_prev = m_ref[h]
            m_new = jnp.maximum(m_prev, jnp.max(m_part, axis=0, keepdims=True))
            alpha = jnp.exp2(m_prev - m_new)
            l_part = jnp.zeros((sub, tq), F32)
            for rows in chunks:
                p = jnp.exp2(s_ref[h, rows, :] - m_new)
                l_part = l_part + jnp.sum(p.reshape(ATT_CHUNK // sub, sub, tq), axis=0)
                p_ref[h, rows, :] = p.astype(BF16)
            l_ref[h] = alpha * l_ref[h] + jnp.sum(l_part, axis=0, keepdims=True)
            acc_ref[h] = alpha * acc_ref[h] + _dot(vt, p_ref[h])
            m_ref[h] = m_new

    @pl.when(j == 0)
    def _():
        m_ref[...] = jnp.full(m_ref.shape, -jnp.inf, F32)
        l_ref[...] = jnp.zeros(l_ref.shape, F32)
        acc_ref[...] = jnp.zeros(acc_ref.shape, F32)
        scores(k0_ref, s0_ref)

    scores(ka_ref, s1_ref)
    softmax_pv(s0_ref, pa_ref, vt_ref[:, :tk])
    scores(kb_ref, s0_ref)
    softmax_pv(s1_ref, pb_ref, vt_ref[:, tk:])

    @pl.when(j == pl.num_programs(2) - 1)
    def _():
        for h in heads:
            o_ref[:, h * HEAD_DIM:(h + 1) * HEAD_DIM] = (acc_ref[h] / l_ref[h]).T.astype(BF16)


def _attention(qt, k, vt, tq=256, tk=1024):
    L = k.shape[0]
    gw = GQA_GROUP * HEAD_DIM
    nk = L // tk
    k_tile = lambda f: pl.BlockSpec((tk, HEAD_DIM), lambda g, i, j: (f(j), g))
    return pl.pallas_call(
        _attn_kernel, grid=(N_KV_HEADS, L // tq, nk // 2),
        in_specs=[pl.BlockSpec((gw, tq), lambda g, i, j: (g, i)),
                  k_tile(lambda j: 0), k_tile(lambda j: 2 * j + 1), k_tile(lambda j: jnp.minimum(2 * j + 2, nk - 1)),
                  pl.BlockSpec((HEAD_DIM, 2 * tk), lambda g, i, j: (g, j))],
        out_specs=pl.BlockSpec((tq, gw), lambda g, i, j: (i, g)),
        out_shape=jax.ShapeDtypeStruct((L, ATT_WIDTH), BF16),
        scratch_shapes=[pltpu.VMEM((GQA_GROUP, 1, tq), F32)] * 2 + [pltpu.VMEM((GQA_GROUP, HEAD_DIM, tq), F32)]
        + [pltpu.VMEM((GQA_GROUP, tk, tq), F32)] * 2 + [pltpu.VMEM((GQA_GROUP, tk, tq), BF16)] * 2,
        compiler_params=_params(3), name="attention")(qt, k, k, k, vt)


HALO = BF16_SUBLANES


def _halo_specs(tm, K, L):
    hb = tm // HALO
    n_h = L // HALO
    return [pl.BlockSpec((tm, K), lambda i, j: (i, 0)),
            pl.BlockSpec((HALO, K), lambda i, j: (jnp.maximum(i * hb - 1, 0), 0)),
            pl.BlockSpec((HALO, K), lambda i, j: (jnp.minimum((i + 1) * hb, n_h - 1), 0))]


def _fill_extended(xe_ref, x_ref, xp_ref, xn_ref):
    tm = x_ref.shape[0]

    @pl.when(pl.program_id(1) == 0)
    def _():
        xe_ref[0:HALO, :] = xp_ref[...]
        xe_ref[HALO:HALO + tm, :] = x_ref[...]
        xe_ref[HALO + tm:, :] = xn_ref[...]


def _conv3_extended(u, w_ref, b_ref):
    n_ext = u.shape[0]
    tm = n_ext - 2 * HALO
    i = pl.program_id(0)
    top = u[:HALO] * jnp.where(i > 0, 1.0, 0.0)
    bot = u[HALO + tm:] * jnp.where(i < pl.num_programs(0) - 1, 1.0, 0.0)
    u = jnp.concatenate([top, u[HALO:HALO + tm], bot], axis=0)
    um = pltpu.roll(u, 1, 0)[HALO:HALO + tm]
    up = pltpu.roll(u, n_ext - 1, 0)[HALO:HALO + tm]
    return w_ref[0:1, :] * um + w_ref[1:2, :] * u[HALO:HALO + tm] + w_ref[2:3, :] * up + b_ref[...]


def _inproj_hy_kernel(x_ref, xp_ref, xn_ref, w0_ref, w1_ref, w2_ref, cw0_ref, cw1_ref, cw2_ref,
                      cb0_ref, cb1_ref, cb2_ref, z_ref, x0_ref, xe_ref):
    _fill_extended(xe_ref, x_ref, xp_ref, xn_ref)
    xe = xe_ref[...]
    x0_ref[...] = _conv3_extended(_dot(xe, w0_ref[...]), cw0_ref, cb0_ref).astype(BF16)
    x1 = _conv3_extended(_dot(xe, w1_ref[...]), cw1_ref, cb1_ref)
    v = _conv3_extended(_dot(xe, w2_ref[...]), cw2_ref, cb2_ref)
    z_ref[...] = (x1 * v).astype(BF16)


def _inproj_hy(xb, w_in_b, l, conv_w, conv_b, tm=1024, tn=256):
    L, K = xb.shape
    C = HYENA_WIDTH
    nj = C // tn
    w_specs = [pl.BlockSpec((None, K, tn), functools.partial(lambda i, j, g: (l, 0, (ATT_IN + g * C) // tn + j), g=g))
               for g in range(3)]
    cw_specs = [pl.BlockSpec((3, tn), functools.partial(lambda i, j, g: (0, g * nj + j), g=g)) for g in range(3)]
    cb_specs = [pl.BlockSpec((1, tn), functools.partial(lambda i, j, g: (0, g * nj + j), g=g)) for g in range(3)]
    out = pl.BlockSpec((tm, tn), lambda i, j: (i, j))
    return pl.pallas_call(
        _inproj_hy_kernel, grid=(L // tm, nj),
        in_specs=_halo_specs(tm, K, L) + w_specs + cw_specs + cb_specs,
        out_specs=[out, out], out_shape=[jax.ShapeDtypeStruct((L, C), BF16)] * 2,
        scratch_shapes=[pltpu.VMEM((tm + 2 * HALO, K), BF16)],
        compiler_params=_params(2), name="inproj_hy")(
            xb, xb, xb, w_in_b, w_in_b, w_in_b, conv_w, conv_w, conv_w, conv_b, conv_b, conv_b)


def _filt_feat_kernel(fb_ref, w1t_ref, w1c_ref, w1s_ref, b1_ref, f1_ref, w2_ref, b2_ref, f2_ref,
                      gf_ref, gr_ref, *, tm):
    pos = lax.broadcasted_iota(jnp.int32, (1, tm), 1) + pl.program_id(0) * tm

    def feats(p):
        pf = p.astype(F32)
        ang = fb_ref[...] * ((2.0 * math.pi * pf) / SEQ)
        pre = (w1t_ref[...] * (pf / (SEQ - 1)) + _dot_hi(w1c_ref[...], jnp.cos(ang))
               + _dot_hi(w1s_ref[...], -jnp.sin(ang)) + b1_ref[...])
        h = jnp.sin(f1_ref[...] * pre)
        return jnp.sin(f2_ref[...] * (_dot_hi(w2_ref[...], h) + b2_ref[...]))

    gf_ref[...] = feats(pos)
    gr_ref[...] = feats(jnp.where(pos == 0, 0, SEQ - pos))


def _filt_feat(fband, w1, b1, f1, w2, b2, f2, tm=1024):
    H = FILTER_HIDDEN
    full = lambda a: pl.BlockSpec(a.shape, lambda i: (0,) * a.ndim)
    col = lambda v: v[:, None]
    args = [col(fband), col(w1[0]), w1[1:1 + FILTER_BANDS].T, w1[1 + FILTER_BANDS:].T, col(b1), col(f1),
            w2.T, col(b2), col(f2)]
    out = pl.BlockSpec((H, tm), lambda i: (0, i))
    return pl.pallas_call(
        functools.partial(_filt_feat_kernel, tm=tm), grid=(SEQ // tm,),
        in_specs=[full(a) for a in args], out_specs=[out, out],
        out_shape=[jax.ShapeDtypeStruct((H, SEQ), F32)] * 2,
        compiler_params=_params(1), name="filt_feat")(*args)


def _dot_t_hi(a_t, b):
    return lax.dot_general(a_t, b, (((0,), (0,)), ((), ())), preferred_element_type=F32,
                           precision=lax.Precision.HIGHEST)


def _filt_gen_kernel(gf_ref, gr_ref, w3f_ref, w3b_ref, b3f_ref, b3b_ref, df_ref, db_ref,
                     kf_ref, kb_ref, norm_ref, *, tm):
    i = pl.program_id(0)
    pos = lax.broadcasted_iota(jnp.int32, (tm, HYENA_WIDTH), 0) + i * tm
    pos_r = jnp.where(pos == 0, 0, SEQ - pos)
    t = pos.astype(F32) / (SEQ - 1)
    tr = pos_r.astype(F32) / (SEQ - 1)
    hf = (_dot_t_hi(gf_ref[...], w3f_ref[...]) + b3f_ref[...]) * jnp.exp(-t * jnp.abs(df_ref[...]))
    hb = (_dot_t_hi(gr_ref[...], w3b_ref[...]) + b3b_ref[...]) * jnp.exp(-tr * jnp.abs(db_ref[...]))

    @pl.when(i == 0)
    def _():
        norm_ref[...] = jnp.zeros(norm_ref.shape, F32)

    norm_ref[...] += jnp.sum(jnp.abs(hf) + jnp.abs(hb), axis=0, keepdims=True)
    kf_ref[...] = hf.astype(BF16)
    kb_ref[...] = jnp.where(pos == 0, 0.0, -hb).astype(BF16)


def _filt_gen(gf, gr, w3, b3, decay, tm=512):
    C, H = HYENA_WIDTH, FILTER_HIDDEN
    g = pl.BlockSpec((H, tm), lambda i: (0, i))
    w = pl.BlockSpec((H, C), lambda i: (0, 0))
    vec = pl.BlockSpec((1, C), lambda i: (0, 0))
    out = pl.BlockSpec((tm, C), lambda i: (i, 0))
    return pl.pallas_call(
        functools.partial(_filt_gen_kernel, tm=tm), grid=(SEQ // tm,),
        in_specs=[g, g, w, w, vec, vec, vec, vec], out_specs=[out, out, vec],
        out_shape=[jax.ShapeDtypeStruct((SEQ, C), BF16)] * 2 + [jax.ShapeDtypeStruct((1, C), F32)],
        compiler_params=_params(1), name="filt_gen")(
            gf, gr, w3[:, :C], w3[:, C:], b3[None, :C], b3[None, C:], decay[0:1], decay[1:2])


def _fft_tables():
    N = FFT_N
    n2 = np.arange(FFT_R, dtype=np.int64)[:, None, None]
    k1 = np.arange(FFT_R // 2, dtype=np.int64)[None, :, None]
    n1 = np.arange(FFT_R, dtype=np.int64)[None, None, :]
    theta = (((FFT_R * n1 + n2) * (2 * k1 + 1)) % (2 * N)) * (math.pi / N)
    c, s = np.cos(theta), np.sin(theta)
    ta = np.concatenate([c, -s], axis=1)
    ti = np.transpose(ta[:, :, :FFT_R // 2], (0, 2, 1)) * (2.0 / N)
    j = np.arange(FFT_R, dtype=np.int64)
    phi = ((j[:, None] * j[None, :]) % FFT_R) * (2.0 * math.pi / FFT_R)
    cm, sm = np.cos(phi), np.sin(phi)
    f2f = np.block([[cm, sm], [-sm, cm]])
    f2i = np.block([[cm, -sm], [sm, cm]])
    return tuple(jnp.asarray(t.astype(np.float32)).astype(BF16) for t in (ta, ti, f2f, f2i))


def _fft_conv_kernel(z_ref, kf_ref, kb_ref, ta_ref, ti_ref, f2f_ref, f2i_ref, y_ref, pz_ref, pk_ref, s_ref):
    half = FFT_R // 2
    tc = z_ref.shape[1]
    zeros_hi = jnp.zeros((half, tc), BF16)

    for c in range(FFT_R // N2_CHUNK):
        off = c * N2_CHUNK
        for n1 in range(half):
            pz_ref[pl.ds(n1 * P_PITCH, N2_CHUNK), :] = z_ref[pl.ds(n1 * FFT_R + off, N2_CHUNK), :].astype(F32)
            pk_ref[pl.ds(n1 * P_PITCH, N2_CHUNK), :] = kf_ref[pl.ds(n1 * FFT_R + off, N2_CHUNK), :].astype(F32)
            pk_ref[pl.ds((half + n1) * P_PITCH, N2_CHUNK), :] = (
                kb_ref[pl.ds(n1 * FFT_R + off, N2_CHUNK), :].astype(F32))

        def stage_a(j, carry):
            n2 = off + j
            xz = pz_ref[pl.ds(j, half, stride=P_PITCH), :].astype(BF16)
            xk = pk_ref[pl.ds(j, FFT_R, stride=P_PITCH), :].astype(BF16)
            rhs = jnp.concatenate([jnp.concatenate([xz, zeros_hi], axis=0), xk], axis=1)
            a = _dot(ta_ref[n2], rhs)
            row = pl.multiple_of(n2 * S_PITCH, 8)
            s_ref[0, pl.ds(row, FFT_R), :] = a[:, :tc]
            s_ref[1, pl.ds(row, FFT_R), :] = a[:, tc:]
            return carry

        lax.fori_loop(0, N2_CHUNK, stage_a, 0, unroll=FFT_UNROLL)

    def rows(k1):
        return pl.ds(k1, FFT_R, stride=S_PITCH), pl.ds(half + k1, FFT_R, stride=S_PITCH)

    def spectrum(k1):
        re_rows, im_rows = rows(k1)
        rhs = jnp.concatenate(
            [jnp.concatenate([s_ref[0, re_rows, :], s_ref[1, re_rows, :]], axis=1),
             jnp.concatenate([s_ref[0, im_rows, :], s_ref[1, im_rows, :]], axis=1)], axis=0).astype(BF16)
        return _dot(f2f_ref[...], rhs)

    def product(x):
        zr, zi = x[:FFT_R, :tc], x[FFT_R:, :tc]
        kr, ki = x[:FFT_R, tc:], x[FFT_R:, tc:]
        return jnp.concatenate([zr * kr - zi * ki, zr * ki + zi * kr], axis=0).astype(BF16)

    def stage_b(i, carry):
        k1s = [i * FFT_KB + u for u in range(FFT_KB)]
        xs = [spectrum(k1) for k1 in k1s]
        for u in range(0, FFT_KB, 2):
            d = _dot(f2i_ref[...], jnp.concatenate([product(xs[u]), product(xs[u + 1])], axis=1))
            for v in range(2):
                re_rows, im_rows = rows(k1s[u + v])
                s_ref[0, re_rows, :] = d[:FFT_R, v * tc:(v + 1) * tc]
                s_ref[0, im_rows, :] = d[FFT_R:, v * tc:(v + 1) * tc]
        return carry

    lax.fori_loop(0, half // FFT_KB, stage_b, 0)

    def stage_out(n2, carry):
        d = s_ref[0, pl.ds(pl.multiple_of(n2 * S_PITCH, 8), FFT_R), :].astype(BF16)
        y_ref[pl.ds(n2, half, stride=FFT_R), :] = _dot(ti_ref[n2], d)
        return carry

    lax.fori_loop(0, FFT_R, stage_out, 0, unroll=FFT_UNROLL)


def _fft_conv(z, kf, kb, tables):
    L, C = z.shape
    tc = FFT_TC
    ta, ti, f2f, f2i = tables
    col = pl.BlockSpec((L, tc), lambda j: (0, j))
    full = lambda a: pl.BlockSpec(a.shape, lambda j: (0,) * a.ndim)
    half = FFT_R // 2
    return pl.pallas_call(
        _fft_conv_kernel, grid=(C // tc,),
        in_specs=[col, col, col, full(ta), full(ti), full(f2f), full(f2i)],
        out_specs=col, out_shape=jax.ShapeDtypeStruct((L, C), F32),
        scratch_shapes=[pltpu.VMEM((half * P_PITCH, tc), F32), pltpu.VMEM((FFT_R * P_PITCH, tc), F32),
                        pltpu.VMEM((2, FFT_R * S_PITCH, tc), F32)],
        compiler_params=_params(1), name="fft_conv")(z, kf, kb, ta, ti, f2f, f2i)


def _outproj_kernel(a_ref, x0_ref, z_ref, yc_ref, norm_ref, skip_ref, ag_ref, hg_ref,
                    wa_ref, wh_ref, b_ref, x_ref, g_ref, beta_ref, xf_ref, xb_ref):
    a = _rms_norm(a_ref[...].astype(F32), ag_ref[...])
    z = z_ref[...].astype(F32)
    hy = x0_ref[...].astype(F32) * (yc_ref[...] * (1.0 / norm_ref[...]) + skip_ref[...] * z)
    h = _rms_norm(hy, hg_ref[...])
    mix = _dot(a.astype(BF16), wa_ref[...]) + _dot(h.astype(BF16), wh_ref[...]) + b_ref[...]
    y = _layer_norm(ALPHA * x_ref[...] + mix, g_ref[...], beta_ref[...])
    xf_ref[...] = y
    xb_ref[...] = y.astype(BF16)


def _outproj(a, x0, z, yc, norm, skip, ag, hg, w_out, l, b_out, x, g, beta, tm=256):
    L, D = x.shape
    C = HYENA_WIDTH
    half = pl.BlockSpec((tm, C), lambda i: (i, 0))
    hvec = pl.BlockSpec((1, C), lambda i: (0, 0))
    row = pl.BlockSpec((tm, D), lambda i: (i, 0))
    vec = pl.BlockSpec((1, D), lambda i: (0, 0))
    wa = pl.BlockSpec((None, ATT_WIDTH, D), lambda i: (l, 0, 0))
    wh = pl.BlockSpec((None, C, D), lambda i: (l, 1, 0))
    return pl.pallas_call(
        _outproj_kernel, grid=(L // tm,),
        in_specs=[half, half, half, half, hvec, hvec, hvec, hvec, wa, wh, vec, row, vec, vec],
        out_specs=[row, row],
        out_shape=[jax.ShapeDtypeStruct((L, D), F32), jax.ShapeDtypeStruct((L, D), BF16)],
        compiler_params=_params(1), name="outproj_ln")(
            a, x0, z, yc, norm, skip, ag, hg, w_out, w_out, b_out, x, g, beta)


def _gelu_tanh(x):
    return x * (0.5 * (1.0 + jnp.tanh(math.sqrt(2.0 / math.pi) * (x + 0.044715 * (x * x * x)))))


def _ffn_up_kernel(x_ref, xp_ref, xn_ref, wg_ref, wv_ref, bg_ref, bv_ref, cw_ref, cb_ref, h_ref, xe_ref):
    _fill_extended(xe_ref, x_ref, xp_ref, xn_ref)
    gate = _conv3_extended(_dot(xe_ref[...], wg_ref[...]) + bg_ref[...], cw_ref, cb_ref)
    val = _dot(x_ref[...], wv_ref[...]) + bv_ref[...]
    h_ref[...] = (_gelu_tanh(gate) * val).astype(BF16)


def _ffn_up(xb, w_gate, w_val, b_gate, b_val, conv_w, conv_b, l, tm=1024, tn=512):
    L, K = xb.shape
    w = pl.BlockSpec((None, K, tn), lambda i, j: (l, 0, j))
    vec = pl.BlockSpec((1, tn), lambda i, j: (0, j))
    return pl.pallas_call(
        _ffn_up_kernel, grid=(L // tm, pl.cdiv(D_FF, tn)),
        in_specs=_halo_specs(tm, K, L) + [w, w, vec, vec, pl.BlockSpec((3, tn), lambda i, j: (0, j)), vec],
        out_specs=pl.BlockSpec((tm, tn), lambda i, j: (i, j)),
        out_shape=jax.ShapeDtypeStruct((L, D_FF), BF16),
        scratch_shapes=[pltpu.VMEM((tm + 2 * HALO, K), BF16)],
        compiler_params=_params(2), name="ffn_up")(xb, xb, xb, w_gate, w_val, b_gate, b_val, conv_w, conv_b)


def _ffn_down_kernel(h_ref, w_ref, b_ref, x_ref, g_ref, beta_ref, xf_ref, xb_ref):
    ffn = _dot(h_ref[...], w_ref[...]) + b_ref[...]
    y = _layer_norm(ALPHA * x_ref[...] + ffn, g_ref[...], beta_ref[...])
    xf_ref[...] = y
    xb_ref[...] = y.astype(BF16)


def _ffn_down(h, w_down, b_down, x, g, beta, l, tm=256):
    L, D = x.shape
    row = pl.BlockSpec((tm, D), lambda i: (i, 0))
    vec = pl.BlockSpec((1, D), lambda i: (0, 0))
    return pl.pallas_call(
        _ffn_down_kernel, grid=(L // tm,),
        in_specs=[pl.BlockSpec((tm, D_FF), lambda i: (i, 0)),
                  pl.BlockSpec((None, D_FF, D), lambda i: (l, 0, 0), pipeline_mode=pl.Buffered(1)),
                  vec, row, vec, vec],
        out_specs=[row, row],
        out_shape=[jax.ShapeDtypeStruct((L, D), F32), jax.ShapeDtypeStruct((L, D), BF16)],
        compiler_params=_params(1), name="ffn_down_ln")(h, w_down, b_down, x, g, beta)


def _rope_tables(L):
    rows = L // GRID_W
    row_pos = np.repeat(np.arange(rows, dtype=np.float64), GRID_W)
    col_pos = np.tile(np.arange(GRID_W, dtype=np.float64), rows)
    dims = HEAD_DIM // 2

    def axis_table(pos):
        inv = ROPE_THETA ** (-np.arange(0, dims, 2, dtype=np.float64) / dims)
        ang = pos[:, None] * inv[None, :]
        ang = np.concatenate([ang, ang], axis=-1)
        return np.cos(ang), np.sin(ang)

    cos_r, sin_r = axis_table(row_pos)
    cos_c, sin_c = axis_table(col_pos)
    cos = np.concatenate([cos_r, cos_c], axis=-1)
    sin = np.concatenate([sin_r, sin_c], axis=-1)
    first = (np.arange(HEAD_DIM) % dims) < dims // 2
    tabs = [t.astype(np.float32) for t in (cos, np.where(first, -sin, 0.0), np.where(first, 0.0, sin))]
    return tuple(jnp.asarray(t) for t in tabs), tuple(jnp.asarray(np.ascontiguousarray(t.T)) for t in tabs)


def kernel(x, ln_in_g, ln_in_b, w_in, q_norm_g, k_norm_g, hy_conv_w, hy_conv_b, filt_w1, filt_b1, filt_f1, filt_w2, filt_b2, filt_f2, filt_w3, filt_b3, filt_decay, hy_skip, att_out_g, hy_out_g, w_out, b_out, ln1_g, ln1_b, w_up, b_up, ffn_conv_w, ffn_conv_b, w_down, b_down, ln2_g, ln2_b):
    assert x.shape == (1, SEQ, D_MODEL) and w_in.shape == (DEPTH, D_MODEL, ATT_IN + HY_IN)
    rope, rope_t = _rope_tables(SEQ)
    tables = _fft_tables()
    fband = jnp.linspace(1e-4, FILTER_BANDS - 1, FILTER_BANDS, dtype=F32)

    w_in_b = w_in.astype(BF16)
    w_out_b = w_out.astype(BF16)
    w_gate_b = w_up[:, :, :D_FF].astype(BF16)
    w_val_b = w_up[:, :, D_FF:].astype(BF16)
    w_down_b = w_down.astype(BF16)

    xf, xb = _input_ln(x[0], ln_in_g[None], ln_in_b[None])
    for l in range(DEPTH):
        qt, k, vt = _inproj_att(xb, w_in_b, l, q_norm_g[l][None], k_norm_g[l][None], rope, rope_t)
        att = _attention(qt, k, vt)
        z, x0 = _inproj_hy(xb, w_in_b, l, hy_conv_w[l], hy_conv_b[l][None])
        gf, gr = _filt_feat(fband, filt_w1[l], filt_b1[l], filt_f1[l], filt_w2[l], filt_b2[l], filt_f2[l])
        kf, kb, norm = _filt_gen(gf, gr, filt_w3[l], filt_b3[l], filt_decay[l])
        yc = _fft_conv(z, kf, kb, tables)
        xf, xb = _outproj(att, x0, z, yc, norm, hy_skip[l][None], att_out_g[l][None], hy_out_g[l][None],
                          w_out_b, l, b_out[l][None], xf, ln1_g[l][None], ln1_b[l][None])
        h = _ffn_up(xb, w_gate_b, w_val_b, b_up[l][None, :D_FF], b_up[l][None, D_FF:],
                    ffn_conv_w[l], ffn_conv_b[l][None], l)
        xf, xb = _ffn_down(h, w_down_b, b_down[l][None], xf, ln2_g[l][None], ln2_b[l][None], l)
    return xf[None]
```

```python
import functools
import math

import jax
import jax.numpy as jnp
import numpy as np
from jax import lax
from jax.experimental import pallas as pl
from jax.experimental.pallas import tpu as pltpu

F32 = jnp.float32
BF16 = jnp.bfloat16

D_MODEL = 2048
SEQ = 8192
DEPTH = 2
ATT_WIDTH = 1024
HYENA_WIDTH = 1024
HEAD_DIM = 128
N_KV_HEADS = 2
GQA_GROUP = 4
KV_WIDTH = N_KV_HEADS * HEAD_DIM
ATT_IN = ATT_WIDTH + 2 * KV_WIDTH
HY_IN = 3 * HYENA_WIDTH
FILTER_HIDDEN = 64
FILTER_BANDS = 16
D_FF = 5504
GRID_W = 64
ROPE_THETA = 10000.0
ALPHA = (2.0 * DEPTH) ** 0.25
LN_EPS = 1e-5
RMS_EPS = 1e-6

LANES = 128
BF16_SUBLANES = 16
VMEM_LIMIT = 56 * 1024 * 1024
ATT_CHUNK = 128

FFT_R = 128
FFT_N = FFT_R * FFT_R
FFT_TC = 128
S_PITCH = 136
N2_CHUNK = 32
P_PITCH = 40
FFT_UNROLL = 8
FFT_KB = 4

_dot = functools.partial(jnp.dot, preferred_element_type=F32)
_dot_hi = functools.partial(jnp.dot, preferred_element_type=F32, precision=lax.Precision.HIGHEST)


def _params(n_axes, flags=None):
    return pltpu.CompilerParams(dimension_semantics=("arbitrary",) * n_axes,
                                vmem_limit_bytes=VMEM_LIMIT, flags=flags)


def _layer_norm(v, g, b):
    mu = jnp.mean(v, axis=-1, keepdims=True)
    d = v - mu
    var = jnp.mean(d * d, axis=-1, keepdims=True)
    return d * lax.rsqrt(var + LN_EPS) * g + b


def _rms_norm(v, g):
    ms = jnp.mean(v * v, axis=-1, keepdims=True)
    return v * lax.rsqrt(ms + RMS_EPS) * g


def _ln_kernel(x_ref, g_ref, b_ref, xf_ref, xb_ref):
    y = _layer_norm(x_ref[...], g_ref[...], b_ref[...])
    xf_ref[...] = y
    xb_ref[...] = y.astype(BF16)


def _input_ln(x, g, b, tm=512):
    L, D = x.shape
    row = pl.BlockSpec((tm, D), lambda i: (i, 0))
    vec = pl.BlockSpec((1, D), lambda i: (0, 0))
    return pl.pallas_call(
        _ln_kernel, grid=(L // tm,), in_specs=[row, vec, vec], out_specs=[row, row],
        out_shape=[jax.ShapeDtypeStruct((L, D), F32), jax.ShapeDtypeStruct((L, D), BF16)],
        compiler_params=_params(1), name="input_ln")(x, g, b)


def _inproj_k_kernel(x_ref, w_ref, g_ref, cos_ref, sa_ref, sb_ref, o_ref):
    acc = _dot(x_ref[...], w_ref[...])
    for h in range(w_ref.shape[1] // HEAD_DIM):
        sl = slice(h * HEAD_DIM, (h + 1) * HEAD_DIM)
        hn = _rms_norm(acc[:, sl], g_ref[...])
        rot = pltpu.roll(hn, HEAD_DIM - 32, 1) * sa_ref[...] + pltpu.roll(hn, 32, 1) * sb_ref[...]
        o_ref[:, sl] = (hn * cos_ref[...] + rot).astype(BF16)


def _inproj_qt_kernel(x_ref, w_ref, g_ref, cos_ref, sa_ref, sb_ref, o_ref, *, scale):
    acc = _dot(x_ref[...], w_ref[...])
    tm = acc.shape[0]
    gain = jnp.tile(g_ref[...], (1, tm // LANES))
    for h in range(w_ref.shape[1] // HEAD_DIM):
        sl = slice(h * HEAD_DIM, (h + 1) * HEAD_DIM)
        xt = acc[:, sl].T
        hn = xt * lax.rsqrt(jnp.mean(xt * xt, axis=0, keepdims=True) + RMS_EPS) * gain
        rot = pltpu.roll(hn, HEAD_DIM - 32, 0) * sa_ref[...] + pltpu.roll(hn, 32, 0) * sb_ref[...]
        o_ref[sl, :] = ((hn * cos_ref[...] + rot) * scale).astype(BF16)


def _inproj_vt_kernel(x_ref, w_ref, o_ref):
    o_ref[...] = _dot(x_ref[...], w_ref[...]).T.astype(BF16)


def _inproj_att(xb, w_in_b, l, qg, kg, tabs, tabs_t, tm=1024):
    cos, sin_a, sin_b = tabs
    L, K = xb.shape
    tn = KV_WIDTH
    tq = 2 * tn
    n_q = ATT_WIDTH // tn
    x_spec = pl.BlockSpec((tm, K), lambda i, j: (i, 0))
    head = pl.BlockSpec((1, HEAD_DIM), lambda i, j: (0, 0))
    tab = pl.BlockSpec((tm, HEAD_DIM), lambda i, j: (i, 0))
    q_scale = HEAD_DIM ** -0.5 * math.log2(math.e)
    tab_t = pl.BlockSpec((HEAD_DIM, tm), lambda i, j: (0, i))
    qt = pl.pallas_call(
        functools.partial(_inproj_qt_kernel, scale=q_scale), grid=(L // tm, ATT_WIDTH // tq),
        in_specs=[x_spec, pl.BlockSpec((None, K, tq), lambda i, j: (l, 0, j)),
                  pl.BlockSpec((HEAD_DIM, LANES), lambda i, j: (0, 0)), tab_t, tab_t, tab_t],
        out_specs=pl.BlockSpec((tq, tm), lambda i, j: (j, i)),
        out_shape=jax.ShapeDtypeStruct((ATT_WIDTH, L), BF16),
        compiler_params=_params(2), name="inproj_q")(
            xb, w_in_b, jnp.broadcast_to(qg.reshape(HEAD_DIM, 1), (HEAD_DIM, LANES)), *tabs_t)
    k = pl.pallas_call(
        _inproj_k_kernel, grid=(L // tm, 1),
        in_specs=[x_spec, pl.BlockSpec((None, K, tn), lambda i, j: (l, 0, n_q)), head, tab, tab, tab],
        out_specs=pl.BlockSpec((tm, tn), lambda i, j: (i, 0)),
        out_shape=jax.ShapeDtypeStruct((L, KV_WIDTH), BF16),
        compiler_params=_params(2), name="inproj_k")(xb, w_in_b, kg, cos, sin_a, sin_b)
    vt = pl.pallas_call(
        _inproj_vt_kernel, grid=(L // tm, 1),
        in_specs=[x_spec, pl.BlockSpec((None, K, tn), lambda i, j: (l, 0, n_q + 1))],
        out_specs=pl.BlockSpec((tn, tm), lambda i, j: (0, i)),
        out_shape=jax.ShapeDtypeStruct((KV_WIDTH, L), BF16),
        compiler_params=_params(2), name="inproj_v")(xb, w_in_b)
    return qt, k, vt


def _attn_kernel(qt_ref, qtn_ref, k0_ref, ka_ref, kb_ref, vt_ref, o_ref, m_ref, l_ref, acc_ref,
                 s0_ref, s1_ref, pa_ref, pb_ref):
    i = pl.program_id(1)
    j = pl.program_id(2)
    last = j == pl.num_programs(2) - 1
    tk, tq = s0_ref.shape[1:]
    sub = 8
    chunks = [pl.ds(c * ATT_CHUNK, ATT_CHUNK) for c in range(tk // ATT_CHUNK)]
    heads = range(GQA_GROUP)

    def scores(k_ref, s_ref, qt):
        k = k_ref[...]
        for h in heads:
            s_ref[h] = _dot(k, qt[h * HEAD_DIM:(h + 1) * HEAD_DIM, :])

    def softmax_pv(s_ref, p_ref, vt):
        for h in heads:
            m_part = jnp.full((sub, tq), -jnp.inf, F32)
            for rows in chunks:
                m_part = jnp.maximum(m_part, jnp.max(s_ref[h, rows, :].reshape(ATT_CHUNK // sub, sub, tq), axis=0))
            m_prev = m_ref[h]
            m_new = jnp.maximum(m_prev, jnp.max(m_part, axis=0, keepdims=True))
            alpha = jnp.exp2(m_prev - m_new)
            l_part = jnp.zeros((sub, tq), F32)
            for rows in chunks:
                p = jnp.exp2(s_ref[h, rows, :] - m_new)
                l_part = l_part + jnp.sum(p.reshape(ATT_CHUNK // sub, sub, tq), axis=0)
                p_ref[h, rows, :] = p.astype(BF16)
            l_ref[h] = alpha * l_ref[h] + jnp.sum(l_part, axis=0, keepdims=True)
            acc_ref[h] = alpha * acc_ref[h] + _dot(vt, p_ref[h])
            m_ref[h] = m_new

    @pl.when(j == 0)
    def _():
        m_ref[...] = jnp.full(m_ref.shape, -jnp.inf, F32)
        l_ref[...] = jnp.zeros(l_ref.shape, F32)
        acc_ref[...] = jnp.zeros(acc_ref.shape, F32)

    @pl.when(jnp.logical_and(j == 0, i == 0))
    def _():
        scores(k0_ref, s0_ref, qt_ref[...])

    qt = qt_ref[...]
    scores(ka_ref, s1_ref, qt)
    softmax_pv(s0_ref, pa_ref, vt_ref[:, :tk])
    scores(kb_ref, s0_ref, jnp.where(last, qtn_ref[...], qt))
    softmax_pv(s1_ref, pb_ref, vt_ref[:, tk:])

    @pl.when(last)
    def _():
        for h in heads:
            o_ref[:, h * HEAD_DIM:(h + 1) * HEAD_DIM] = (acc_ref[h] / l_ref[h]).T.astype(BF16)


def _attention(qt, k, vt, tq=256, tk=1024):
    L = k.shape[0]
    gw = GQA_GROUP * HEAD_DIM
    nk = L // tk
    nq = L // tq
    k_tile = lambda f: pl.BlockSpec((tk, HEAD_DIM), lambda g, i, j: (f(j), g))
    return pl.pallas_call(
        _attn_kernel, grid=(N_KV_HEADS, nq, nk // 2),
        in_specs=[pl.BlockSpec((gw, tq), lambda g, i, j: (g, i)),
                  pl.BlockSpec((gw, tq), lambda g, i, j: (g, jnp.minimum(i + 1, nq - 1))),
                  k_tile(lambda j: 0), k_tile(lambda j: 2 * j + 1), k_tile(lambda j: lax.rem(2 * j + 2, nk)),
                  pl.BlockSpec((HEAD_DIM, 2 * tk), lambda g, i, j: (g, j))],
        out_specs=pl.BlockSpec((tq, gw), lambda g, i, j: (i, g)),
        out_shape=jax.ShapeDtypeStruct((L, ATT_WIDTH), BF16),
        scratch_shapes=[pltpu.VMEM((GQA_GROUP, 1, tq), F32)] * 2 + [pltpu.VMEM((GQA_GROUP, HEAD_DIM, tq), F32)]
        + [pltpu.VMEM((GQA_GROUP, tk, tq), F32)] * 2 + [pltpu.VMEM((GQA_GROUP, tk, tq), BF16)] * 2,
        compiler_params=_params(3), name="attention")(qt, qt, k, k, k, vt)


HALO = BF16_SUBLANES


def _halo_specs(tm, K, L):
    hb = tm // HALO
    n_h = L // HALO
    return [pl.BlockSpec((tm, K), lambda i, j: (i, 0)),
            pl.BlockSpec((HALO, K), lambda i, j: (jnp.maximum(i * hb - 1, 0), 0)),
            pl.BlockSpec((HALO, K), lambda i, j: (jnp.minimum((i + 1) * hb, n_h - 1), 0))]


def _fill_extended(xe_ref, x_ref, xp_ref, xn_ref):
    tm = x_ref.shape[0]

    @pl.when(pl.program_id(1) == 0)
    def _():
        xe_ref[0:HALO, :] = xp_ref[...]
        xe_ref[HALO:HALO + tm, :] = x_ref[...]
        xe_ref[HALO + tm:, :] = xn_ref[...]


def _conv3_extended(u, w_ref, b_ref):
    n_ext = u.shape[0]
    tm = n_ext - 2 * HALO
    i = pl.program_id(0)
    top = u[:HALO] * jnp.where(i > 0, 1.0, 0.0)
    bot = u[HALO + tm:] * jnp.where(i < pl.num_programs(0) - 1, 1.0, 0.0)
    u = jnp.concatenate([top, u[HALO:HALO + tm], bot], axis=0)
    um = pltpu.roll(u, 1, 0)[HALO:HALO + tm]
    up = pltpu.roll(u, n_ext - 1, 0)[HALO:HALO + tm]
    return w_ref[0:1, :] * um + w_ref[1:2, :] * u[HALO:HALO + tm] + w_ref[2:3, :] * up + b_ref[...]


def _inproj_hy_kernel(x_ref, xp_ref, xn_ref, w0_ref, w1_ref, w2_ref, cw0_ref, cw1_ref, cw2_ref,
                      cb0_ref, cb1_ref, cb2_ref, z_ref, x0_ref, xe_ref):
    _fill_extended(xe_ref, x_ref, xp_ref, xn_ref)
    xe = xe_ref[...]
    x0_ref[...] = _conv3_extended(_dot(xe, w0_ref[...]), cw0_ref, cb0_ref).astype(BF16)
    x1 = _conv3_extended(_dot(xe, w1_ref[...]), cw1_ref, cb1_ref)
    v = _conv3_extended(_dot(xe, w2_ref[...]), cw2_ref, cb2_ref)
    z_ref[...] = (x1 * v).astype(BF16)


def _inproj_hy(xb, w_in_b, l, conv_w, conv_b, tm=1024, tn=512):
    L, K = xb.shape
    C = HYENA_WIDTH
    nj = C // tn
    w_specs = [pl.BlockSpec((None, K, tn), functools.partial(lambda i, j, g: (l, 0, (ATT_IN + g * C) // tn + j), g=g))
               for g in range(3)]
    cw_specs = [pl.BlockSpec((3, tn), functools.partial(lambda i, j, g: (0, g * nj + j), g=g)) for g in range(3)]
    cb_specs = [pl.BlockSpec((1, tn), functools.partial(lambda i, j, g: (0, g * nj + j), g=g)) for g in range(3)]
    out = pl.BlockSpec((tm, tn), lambda i, j: (i, j))
    return pl.pallas_call(
        _inproj_hy_kernel, grid=(L // tm, nj),
        in_specs=_halo_specs(tm, K, L) + w_specs + cw_specs + cb_specs,
        out_specs=[out, out], out_shape=[jax.ShapeDtypeStruct((L, C), BF16)] * 2,
        scratch_shapes=[pltpu.VMEM((tm + 2 * HALO, K), BF16)],
        compiler_params=_params(2), name="inproj_hy")(
            xb, xb, xb, w_in_b, w_in_b, w_in_b, conv_w, conv_w, conv_w, conv_b, conv_b, conv_b)


def _filt_feat_kernel(fb_ref, w1t_ref, w1c_ref, w1s_ref, b1_ref, f1_ref, w2_ref, b2_ref, f2_ref,
                      gf_ref, gr_ref, *, tm):
    pos = lax.broadcasted_iota(jnp.int32, (1, tm), 1) + pl.program_id(0) * tm

    def feats(p):
        pf = p.astype(F32)
        ang = fb_ref[...] * ((2.0 * math.pi * pf) / SEQ)
        pre = (w1t_ref[...] * (pf / (SEQ - 1)) + _dot_hi(w1c_ref[...], jnp.cos(ang))
               + _dot_hi(w1s_ref[...], -jnp.sin(ang)) + b1_ref[...])
        h = jnp.sin(f1_ref[...] * pre)
        return jnp.sin(f2_ref[...] * (_dot_hi(w2_ref[...], h) + b2_ref[...]))

    gf_ref[...] = feats(pos)
    gr_ref[...] = feats(jnp.where(pos == 0, 0, SEQ - pos))


def _filt_feat(fband, w1, b1, f1, w2, b2, f2, tm=1024):
    H = FILTER_HIDDEN
    full = lambda a: pl.BlockSpec(a.shape, lambda i: (0,) * a.ndim)
    col = lambda v: v[:, None]
    args = [col(fband), col(w1[0]), w1[1:1 + FILTER_BANDS].T, w1[1 + FILTER_BANDS:].T, col(b1), col(f1),
            w2.T, col(b2), col(f2)]
    out = pl.BlockSpec((H, tm), lambda i: (0, i))
    return pl.pallas_call(
        functools.partial(_filt_feat_kernel, tm=tm), grid=(SEQ // tm,),
        in_specs=[full(a) for a in args], out_specs=[out, out],
        out_shape=[jax.ShapeDtypeStruct((H, SEQ), F32)] * 2,
        compiler_params=_params(1), name="filt_feat")(*args)


def _dot_t(a_t, b):
    return lax.dot_general(a_t.astype(BF16), b.astype(BF16), (((0,), (0,)), ((), ())), preferred_element_type=F32)


def _filt_gen_kernel(gf_ref, gr_ref, w3f_ref, w3b_ref, b3f_ref, b3b_ref, df_ref, db_ref,
                     kf_ref, kb_ref, norm_ref, *, tm):
    i = pl.program_id(0)
    pos = lax.broadcasted_iota(jnp.int32, (tm, HYENA_WIDTH), 0) + i * tm
    pos_r = jnp.where(pos == 0, 0, SEQ - pos)
    t = pos.astype(F32) / (SEQ - 1)
    tr = pos_r.astype(F32) / (SEQ - 1)
    hf = (_dot_t(gf_ref[...], w3f_ref[...]) + b3f_ref[...]) * jnp.exp(-t * jnp.abs(df_ref[...]))
    hb = (_dot_t(gr_ref[...], w3b_ref[...]) + b3b_ref[...]) * jnp.exp(-tr * jnp.abs(db_ref[...]))

    @pl.when(i == 0)
    def _():
        norm_ref[...] = jnp.zeros(norm_ref.shape, F32)

    norm_ref[...] += jnp.sum(jnp.abs(hf) + jnp.abs(hb), axis=0, keepdims=True)
    kf_ref[...] = hf.astype(BF16)
    kb_ref[...] = jnp.where(pos == 0, 0.0, -hb).astype(BF16)


def _filt_gen(gf, gr, w3, b3, decay, tm=512):
    C, H = HYENA_WIDTH, FILTER_HIDDEN
    g = pl.BlockSpec((H, tm), lambda i: (0, i))
    w = pl.BlockSpec((H, C), lambda i: (0, 0))
    vec = pl.BlockSpec((1, C), lambda i: (0, 0))
    out = pl.BlockSpec((tm, C), lambda i: (i, 0))
    return pl.pallas_call(
        functools.partial(_filt_gen_kernel, tm=tm), grid=(SEQ // tm,),
        in_specs=[g, g, w, w, vec, vec, vec, vec], out_specs=[out, out, vec],
        out_shape=[jax.ShapeDtypeStruct((SEQ, C), BF16)] * 2 + [jax.ShapeDtypeStruct((1, C), F32)],
        compiler_params=_params(1), name="filt_gen")(
            gf, gr, w3[:, :C], w3[:, C:], b3[None, :C], b3[None, C:], decay[0:1], decay[1:2])


def _fft_tables():
    N = FFT_N
    n2 = np.arange(FFT_R, dtype=np.int64)[:, None, None]
    k1 = np.arange(FFT_R // 2, dtype=np.int64)[None, :, None]
    n1 = np.arange(FFT_R, dtype=np.int64)[None, None, :]
    theta = (((FFT_R * n1 + n2) * (2 * k1 + 1)) % (2 * N)) * (math.pi / N)
    c, s = np.cos(theta), np.sin(theta)
    ta = np.concatenate([c, -s], axis=1)
    ti = np.transpose(ta[:, :, :FFT_R // 2], (0, 2, 1)) * (2.0 / N)
    j = np.arange(FFT_R, dtype=np.int64)
    phi = ((j[:, None] * j[None, :]) % FFT_R) * (2.0 * math.pi / FFT_R)
    cm, sm = np.cos(phi), np.sin(phi)
    f2f = np.block([[cm, sm], [-sm, cm]])
    f2i = np.block([[cm, -sm], [sm, cm]])
    return tuple(jnp.asarray(t.astype(np.float32)).astype(BF16) for t in (ta, ti, f2f, f2i))


def _fft_conv_kernel(z_ref, kf_ref, kb_ref, ta_ref, ti_ref, f2f_ref, f2i_ref, y_ref, pz_ref, pk_ref, s_ref):
    half = FFT_R // 2
    tc = z_ref.shape[1]
    zeros_hi = jnp.zeros((half, tc), BF16)

    for c in range(FFT_R // N2_CHUNK):
        off = c * N2_CHUNK
        for n1 in range(half):
            pz_ref[pl.ds(n1 * P_PITCH, N2_CHUNK), :] = z_ref[pl.ds(n1 * FFT_R + off, N2_CHUNK), :].astype(F32)
            pk_ref[pl.ds(n1 * P_PITCH, N2_CHUNK), :] = kf_ref[pl.ds(n1 * FFT_R + off, N2_CHUNK), :].astype(F32)
            pk_ref[pl.ds((half + n1) * P_PITCH, N2_CHUNK), :] = (
                kb_ref[pl.ds(n1 * FFT_R + off, N2_CHUNK), :].astype(F32))

        def stage_a(j, carry):
            n2 = off + j
            xz = pz_ref[pl.ds(j, half, stride=P_PITCH), :].astype(BF16)
            xk = pk_ref[pl.ds(j, FFT_R, stride=P_PITCH), :].astype(BF16)
            rhs = jnp.concatenate([jnp.concatenate([xz, zeros_hi], axis=0), xk], axis=1)
            a = _dot(ta_ref[n2], rhs)
            row = pl.multiple_of(n2 * S_PITCH, 8)
            s_ref[0, pl.ds(row, FFT_R), :] = a[:, :tc]
            s_ref[1, pl.ds(row, FFT_R), :] = a[:, tc:]
            return carry

        lax.fori_loop(0, N2_CHUNK, stage_a, 0, unroll=FFT_UNROLL)

    def rows(k1):
        return pl.ds(k1, FFT_R, stride=S_PITCH), pl.ds(half + k1, FFT_R, stride=S_PITCH)

    def spectrum(k1):
        re_rows, im_rows = rows(k1)
        rhs = jnp.concatenate(
            [jnp.concatenate([s_ref[0, re_rows, :], s_ref[1, re_rows, :]], axis=1),
             jnp.concatenate([s_ref[0, im_rows, :], s_ref[1, im_rows, :]], axis=1)], axis=0).astype(BF16)
        return _dot(f2f_ref[...], rhs)

    def product(x):
        zr, zi = x[:FFT_R, :tc], x[FFT_R:, :tc]
        kr, ki = x[:FFT_R, tc:], x[FFT_R:, tc:]
        return jnp.concatenate([zr * kr - zi * ki, zr * ki + zi * kr], axis=0).astype(BF16)

    def stage_b(i, carry):
        k1s = [i * FFT_KB + u for u in range(FFT_KB)]
        xs = [spectrum(k1) for k1 in k1s]
        for u in range(0, FFT_KB, 2):
            d = _dot(f2i_ref[...], jnp.concatenate([product(xs[u]), product(xs[u + 1])], axis=1))
            for v in range(2):
                re_rows, im_rows = rows(k1s[u + v])
                s_ref[0, re_rows, :] = d[:FFT_R, v * tc:(v + 1) * tc]
                s_ref[0, im_rows, :] = d[FFT_R:, v * tc:(v + 1) * tc]
        return carry

    lax.fori_loop(0, half // FFT_KB, stage_b, 0)

    def stage_out(n2, carry):
        d = s_ref[0, pl.ds(pl.multiple_of(n2 * S_PITCH, 8), FFT_R), :].astype(BF16)
        y_ref[pl.ds(n2, half, stride=FFT_R), :] = _dot(ti_ref[n2], d)
        return carry

    lax.fori_loop(0, FFT_R, stage_out, 0, unroll=FFT_UNROLL)


def _fft_conv(z, kf, kb, tables):
    L, C = z.shape
    tc = FFT_TC
    ta, ti, f2f, f2i = tables
    col = pl.BlockSpec((L, tc), lambda j: (0, j))
    full = lambda a: pl.BlockSpec(a.shape, lambda j: (0,) * a.ndim)
    half = FFT_R // 2
    return pl.pallas_call(
        _fft_conv_kernel, grid=(C // tc,),
        in_specs=[col, col, col, full(ta), full(ti), full(f2f), full(f2i)],
        out_specs=col, out_shape=jax.ShapeDtypeStruct((L, C), F32),
        scratch_shapes=[pltpu.VMEM((half * P_PITCH, tc), F32), pltpu.VMEM((FFT_R * P_PITCH, tc), F32),
                        pltpu.VMEM((2, FFT_R * S_PITCH, tc), F32)],
        compiler_params=_params(1), name="fft_conv")(z, kf, kb, ta, ti, f2f, f2i)


def _outproj_kernel(a_ref, x0_ref, z_ref, yc_ref, norm_ref, skip_ref, ag_ref, hg_ref,
                    w_ref, b_ref, x_ref, g_ref, beta_ref, xf_ref, xb_ref, *, sub_rows):
    inv_norm = 1.0 / norm_ref[...]
    for r in range(a_ref.shape[0] // sub_rows):
        rows = pl.ds(r * sub_rows, sub_rows)
        a = _rms_norm(a_ref[rows, :].astype(F32), ag_ref[...])
        z = z_ref[rows, :].astype(F32)
        hy = x0_ref[rows, :].astype(F32) * (yc_ref[rows, :] * inv_norm + skip_ref[...] * z)
        h = _rms_norm(hy, hg_ref[...])
        mix = _dot(jnp.concatenate([a.astype(BF16), h.astype(BF16)], axis=1), w_ref[...]) + b_ref[...]
        y = _layer_norm(ALPHA * x_ref[rows, :] + mix, g_ref[...], beta_ref[...])
        xf_ref[rows, :] = y
        xb_ref[rows, :] = y.astype(BF16)


def _outproj(a, x0, z, yc, norm, skip, ag, hg, w_out, l, b_out, x, g, beta, tm=512, sub_rows=128):
    L, D = x.shape
    C = HYENA_WIDTH
    half = pl.BlockSpec((tm, C), lambda i: (i, 0))
    hvec = pl.BlockSpec((1, C), lambda i: (0, 0))
    row = pl.BlockSpec((tm, D), lambda i: (i, 0))
    vec = pl.BlockSpec((1, D), lambda i: (0, 0))
    w = pl.BlockSpec((None, ATT_WIDTH + C, D), lambda i: (l, 0, 0), pipeline_mode=pl.Buffered(1))
    return pl.pallas_call(
        functools.partial(_outproj_kernel, sub_rows=sub_rows), grid=(L // tm,),
        in_specs=[half, half, half, half, hvec, hvec, hvec, hvec, w, vec, row, vec, vec],
        out_specs=[row, row],
        out_shape=[jax.ShapeDtypeStruct((L, D), F32), jax.ShapeDtypeStruct((L, D), BF16)],
        compiler_params=_params(1), name="outproj_ln")(
            a, x0, z, yc, norm, skip, ag, hg, w_out, b_out, x, g, beta)


def _gelu_tanh(x):
    return x * (0.5 * (1.0 + jnp.tanh(math.sqrt(2.0 / math.pi) * (x + 0.044715 * (x * x * x)))))


def _ffn_up_kernel(x_ref, xp_ref, xn_ref, wg_ref, wv_ref, bg_ref, bv_ref, cw_ref, cb_ref, h_ref, xe_ref):
    _fill_extended(xe_ref, x_ref, xp_ref, xn_ref)
    gate = _conv3_extended(_dot(xe_ref[...], wg_ref[...]) + bg_ref[...], cw_ref, cb_ref)
    val = _dot(x_ref[...], wv_ref[...]) + bv_ref[...]
    h_ref[...] = (_gelu_tanh(gate) * val).astype(BF16)


def _ffn_up(xb, w_gate, w_val, b_gate, b_val, conv_w, conv_b, l, tm=1024, tn=512):
    L, K = xb.shape
    w = pl.BlockSpec((None, K, tn), lambda i, j: (l, 0, j))
    vec = pl.BlockSpec((1, tn), lambda i, j: (0, j))
    return pl.pallas_call(
        _ffn_up_kernel, grid=(L // tm, pl.cdiv(D_FF, tn)),
        in_specs=_halo_specs(tm, K, L) + [w, w, vec, vec, pl.BlockSpec((3, tn), lambda i, j: (0, j)), vec],
        out_specs=pl.BlockSpec((tm, tn), lambda i, j: (i, j)),
        out_shape=jax.ShapeDtypeStruct((L, D_FF), BF16),
        scratch_shapes=[pltpu.VMEM((tm + 2 * HALO, K), BF16)],
        compiler_params=_params(2), name="ffn_up")(xb, xb, xb, w_gate, w_val, b_gate, b_val, conv_w, conv_b)


def _ffn_down_kernel(h_ref, w_ref, b_ref, x_ref, g_ref, beta_ref, xf_ref, xb_ref):
    ffn = _dot(h_ref[...], w_ref[...]) + b_ref[...]
    y = _layer_norm(ALPHA * x_ref[...] + ffn, g_ref[...], beta_ref[...])
    xf_ref[...] = y
    xb_ref[...] = y.astype(BF16)


def _ffn_down(h, w_down, b_down, x, g, beta, l, tm=256):
    L, D = x.shape
    row = pl.BlockSpec((tm, D), lambda i: (i, 0))
    vec = pl.BlockSpec((1, D), lambda i: (0, 0))
    return pl.pallas_call(
        _ffn_down_kernel, grid=(L // tm,),
        in_specs=[pl.BlockSpec((tm, D_FF), lambda i: (i, 0)),
                  pl.BlockSpec((None, D_FF, D), lambda i: (l, 0, 0), pipeline_mode=pl.Buffered(1)),
                  vec, row, vec, vec],
        out_specs=[row, row],
        out_shape=[jax.ShapeDtypeStruct((L, D), F32), jax.ShapeDtypeStruct((L, D), BF16)],
        compiler_params=_params(1), name="ffn_down_ln")(h, w_down, b_down, x, g, beta)


def _rope_tables(L):
    rows = L // GRID_W
    row_pos = np.repeat(np.arange(rows, dtype=np.float64), GRID_W)
    col_pos = np.tile(np.arange(GRID_W, dtype=np.float64), rows)
    dims = HEAD_DIM // 2

    def axis_table(pos):
        inv = ROPE_THETA ** (-np.arange(0, dims, 2, dtype=np.float64) / dims)
        ang = pos[:, None] * inv[None, :]
        ang = np.concatenate([ang, ang], axis=-1)
        return np.cos(ang), np.sin(ang)

    cos_r, sin_r = axis_table(row_pos)
    cos_c, sin_c = axis_table(col_pos)
    cos = np.concatenate([cos_r, cos_c], axis=-1)
    sin = np.concatenate([sin_r, sin_c], axis=-1)
    first = (np.arange(HEAD_DIM) % dims) < dims // 2
    tabs = [t.astype(np.float32) for t in (cos, np.where(first, -sin, 0.0), np.where(first, 0.0, sin))]
    return tuple(jnp.asarray(t) for t in tabs), tuple(jnp.asarray(np.ascontiguousarray(t.T)) for t in tabs)


def kernel(x, ln_in_g, ln_in_b, w_in, q_norm_g, k_norm_g, hy_conv_w, hy_conv_b, filt_w1, filt_b1, filt_f1, filt_w2, filt_b2, filt_f2, filt_w3, filt_b3, filt_decay, hy_skip, att_out_g, hy_out_g, w_out, b_out, ln1_g, ln1_b, w_up, b_up, ffn_conv_w, ffn_conv_b, w_down, b_down, ln2_g, ln2_b):
    assert x.shape == (1, SEQ, D_MODEL) and w_in.shape == (DEPTH, D_MODEL, ATT_IN + HY_IN)
    rope, rope_t = _rope_tables(SEQ)
    tables = _fft_tables()
    fband = jnp.linspace(1e-4, FILTER_BANDS - 1, FILTER_BANDS, dtype=F32)

    w_in_b = w_in.astype(BF16)
    w_out_b = w_out.astype(BF16)
    w_gate_b = w_up[:, :, :D_FF].astype(BF16)
    w_val_b = w_up[:, :, D_FF:].astype(BF16)
    w_down_b = w_down.astype(BF16)

    xf, xb = _input_ln(x[0], ln_in_g[None], ln_in_b[None])
    for l in range(DEPTH):
        qt, k, vt = _inproj_att(xb, w_in_b, l, q_norm_g[l][None], k_norm_g[l][None], rope, rope_t)
        att = _attention(qt, k, vt)
        z, x0 = _inproj_hy(xb, w_in_b, l, hy_conv_w[l], hy_conv_b[l][None])
        gf, gr = _filt_feat(fband, filt_w1[l], filt_b1[l], filt_f1[l], filt_w2[l], filt_b2[l], filt_f2[l])
        kf, kb, norm = _filt_gen(gf, gr, filt_w3[l], filt_b3[l], filt_decay[l])
        yc = _fft_conv(z, kf, kb, tables)
        xf, xb = _outproj(att, x0, z, yc, norm, hy_skip[l][None], att_out_g[l][None], hy_out_g[l][None],
                          w_out_b, l, b_out[l][None], xf, ln1_g[l][None], ln1_b[l][None])
        h = _ffn_up(xb, w_gate_b, w_val_b, b_up[l][None, :D_FF], b_up[l][None, D_FF:],
                    ffn_conv_w[l], ffn_conv_b[l][None], l)
        xf, xb = _ffn_down(h, w_down_b, b_down[l][None], xf, ln2_g[l][None], ln2_b[l][None], l)
    return xf[None]
```

```python
import functools
import math

import jax
import jax.numpy as jnp
import numpy as np
from jax import lax
from jax.experimental import pallas as pl
from jax.experimental.pallas import tpu as pltpu

F32 = jnp.float32
BF16 = jnp.bfloat16

D_MODEL = 2048
SEQ = 8192
DEPTH = 2
ATT_WIDTH = 1024
HYENA_WIDTH = 1024
HEAD_DIM = 128
N_KV_HEADS = 2
GQA_GROUP = 4
KV_WIDTH = N_KV_HEADS * HEAD_DIM
ATT_IN = ATT_WIDTH + 2 * KV_WIDTH
HY_IN = 3 * HYENA_WIDTH
FILTER_HIDDEN = 64
FILTER_BANDS = 16
D_FF = 5504
GRID_W = 64
ROPE_THETA = 10000.0
ALPHA = (2.0 * DEPTH) ** 0.25
LN_EPS = 1e-5
RMS_EPS = 1e-6

LANES = 128
BF16_SUBLANES = 16
VMEM_LIMIT = 56 * 1024 * 1024
ATT_CHUNK = 128

FFT_R = 128
FFT_N = FFT_R * FFT_R
FFT_TC = 128
S_PITCH = 136
N2_CHUNK = 32
P_PITCH = 40
FFT_UNROLL = 8
FFT_KB = 16

_dot = functools.partial(jnp.dot, preferred_element_type=F32)
_dot_hi = functools.partial(jnp.dot, preferred_element_type=F32, precision=lax.Precision.HIGHEST)


def _params(n_axes, flags=None):
    return pltpu.CompilerParams(dimension_semantics=("arbitrary",) * n_axes,
                                vmem_limit_bytes=VMEM_LIMIT, flags=flags)


def _layer_norm(v, g, b):
    mu = jnp.mean(v, axis=-1, keepdims=True)
    d = v - mu
    var = jnp.mean(d * d, axis=-1, keepdims=True)
    return d * lax.rsqrt(var + LN_EPS) * g + b


def _rms_norm(v, g):
    ms = jnp.mean(v * v, axis=-1, keepdims=True)
    return v * lax.rsqrt(ms + RMS_EPS) * g


def _ln_kernel(x_ref, g_ref, b_ref, xf_ref, xb_ref):
    y = _layer_norm(x_ref[...], g_ref[...], b_ref[...])
    xf_ref[...] = y
    xb_ref[...] = y.astype(BF16)


def _input_ln(x, g, b, tm=512):
    L, D = x.shape
    row = pl.BlockSpec((tm, D), lambda i: (i, 0))
    vec = pl.BlockSpec((1, D), lambda i: (0, 0))
    return pl.pallas_call(
        _ln_kernel, grid=(L // tm,), in_specs=[row, vec, vec], out_specs=[row, row],
        out_shape=[jax.ShapeDtypeStruct((L, D), F32), jax.ShapeDtypeStruct((L, D), BF16)],
        compiler_params=_params(1), name="input_ln")(x, g, b)


def _inproj_head_kernel(x_ref, w_ref, g_ref, cos_ref, sa_ref, sb_ref, o_ref, *, scale, transposed_out):
    acc = _dot(x_ref[...], w_ref[...])
    tm = acc.shape[0]
    gain = jnp.tile(g_ref[...], (1, tm // LANES))
    for h in range(w_ref.shape[1] // HEAD_DIM):
        sl = slice(h * HEAD_DIM, (h + 1) * HEAD_DIM)
        xt = acc[:, sl].T
        hn = xt * lax.rsqrt(jnp.mean(xt * xt, axis=0, keepdims=True) + RMS_EPS) * gain
        rot = pltpu.roll(hn, HEAD_DIM - 32, 0) * sa_ref[...] + pltpu.roll(hn, 32, 0) * sb_ref[...]
        r = hn * cos_ref[...] + rot
        if transposed_out:
            o_ref[sl, :] = (r * scale).astype(BF16)
        else:
            o_ref[:, sl] = r.T.astype(BF16)


def _inproj_vt_kernel(x_ref, w_ref, o_ref):
    o_ref[...] = _dot(x_ref[...], w_ref[...]).T.astype(BF16)


def _inproj_att(xb, w_in_b, l, qg, kg, tabs_t, tm=1024):
    L, K = xb.shape
    tn = KV_WIDTH
    tq = 2 * tn
    n_q = ATT_WIDTH // tn
    x_spec = pl.BlockSpec((tm, K), lambda i, j: (i, 0))
    gain = pl.BlockSpec((HEAD_DIM, LANES), lambda i, j: (0, 0))
    gain_col = lambda g: jnp.broadcast_to(g.reshape(HEAD_DIM, 1), (HEAD_DIM, LANES))
    tab_t = pl.BlockSpec((HEAD_DIM, tm), lambda i, j: (0, i))
    q_scale = HEAD_DIM ** -0.5 * math.log2(math.e)
    qt = pl.pallas_call(
        functools.partial(_inproj_head_kernel, scale=q_scale, transposed_out=True), grid=(L // tm, ATT_WIDTH // tq),
        in_specs=[x_spec, pl.BlockSpec((None, K, tq), lambda i, j: (l, 0, j)), gain, tab_t, tab_t, tab_t],
        out_specs=pl.BlockSpec((tq, tm), lambda i, j: (j, i)),
        out_shape=jax.ShapeDtypeStruct((ATT_WIDTH, L), BF16),
        compiler_params=_params(2), name="inproj_q")(xb, w_in_b, gain_col(qg), *tabs_t)
    k = pl.pallas_call(
        functools.partial(_inproj_head_kernel, scale=1.0, transposed_out=False), grid=(L // tm, 1),
        in_specs=[x_spec, pl.BlockSpec((None, K, tn), lambda i, j: (l, 0, n_q)), gain, tab_t, tab_t, tab_t],
        out_specs=pl.BlockSpec((tm, tn), lambda i, j: (i, 0)),
        out_shape=jax.ShapeDtypeStruct((L, KV_WIDTH), BF16),
        compiler_params=_params(2), name="inproj_k")(xb, w_in_b, gain_col(kg), *tabs_t)
    vt = pl.pallas_call(
        _inproj_vt_kernel, grid=(L // tm, 1),
        in_specs=[x_spec, pl.BlockSpec((None, K, tn), lambda i, j: (l, 0, n_q + 1))],
        out_specs=pl.BlockSpec((tn, tm), lambda i, j: (0, i)),
        out_shape=jax.ShapeDtypeStruct((KV_WIDTH, L), BF16),
        compiler_params=_params(2), name="inproj_v")(xb, w_in_b)
    return qt, k, vt


def _attn_kernel(qt_ref, qtn_ref, *refs, tiles, ahead):
    k_first, k_next = refs[:ahead], refs[ahead:ahead + tiles]
    vt_ref, o_ref, m_ref, l_ref, acc_ref = refs[ahead + tiles:ahead + tiles + 5]
    s_refs = refs[ahead + tiles + 5:ahead + 2 * tiles + 5]
    p_refs = refs[ahead + 2 * tiles + 5:]
    i = pl.program_id(1)
    j = pl.program_id(2)
    last = j == pl.num_programs(2) - 1
    tk, tq = s_refs[0].shape[1:]
    sub = 8
    chunks = [pl.ds(c * ATT_CHUNK, ATT_CHUNK) for c in range(tk // ATT_CHUNK)]
    heads = range(GQA_GROUP)

    def scores(k_ref, s_ref, qt):
        k = k_ref[...]
        for h in heads:
            s_ref[h] = _dot(k, qt[h * HEAD_DIM:(h + 1) * HEAD_DIM, :])

    def softmax_pv(s_ref, p_ref, vt):
        for h in heads:
            m_part = jnp.full((sub, tq), -jnp.inf, F32)
            for rows in chunks:
                m_part = jnp.maximum(m_part, jnp.max(s_ref[h, rows, :].reshape(ATT_CHUNK // sub, sub, tq), axis=0))
            m_prev = m_ref[h]
            m_new = jnp.maximum(m_prev, jnp.max(m_part, axis=0, keepdims=True))
            alpha = jnp.exp2(m_prev - m_new)
            l_part = jnp.zeros((sub, tq), F32)
            for rows in chunks:
                p = jnp.exp2(s_ref[h, rows, :] - m_new)
                l_part = l_part + jnp.sum(p.reshape(ATT_CHUNK // sub, sub, tq), axis=0)
                p_ref[h, rows, :] = p.astype(BF16)
            l_ref[h] = alpha * l_ref[h] + jnp.sum(l_part, axis=0, keepdims=True)
            acc_ref[h] = alpha * acc_ref[h] + _dot(vt, p_ref[h])
            m_ref[h] = m_new

    @pl.when(j == 0)
    def _():
        m_ref[...] = jnp.full(m_ref.shape, -jnp.inf, F32)
        l_ref[...] = jnp.zeros(l_ref.shape, F32)
        acc_ref[...] = jnp.zeros(acc_ref.shape, F32)

    @pl.when(jnp.logical_and(j == 0, i == 0))
    def _():
        for r in range(ahead):
            scores(k_first[r], s_refs[r], qt_ref[...])

    qt = qt_ref[...]
    qt_next = jnp.where(last, qtn_ref[...], qt)
    for r in range(tiles):
        scores(k_next[r], s_refs[(r + ahead) % tiles], qt if r + ahead < tiles else qt_next)
        softmax_pv(s_refs[r], p_refs[r], vt_ref[:, r * tk:(r + 1) * tk])

    @pl.when(last)
    def _():
        for h in heads:
            o_ref[:, h * HEAD_DIM:(h + 1) * HEAD_DIM] = (acc_ref[h] / l_ref[h]).T.astype(BF16)


def _attention(qt, k, vt, tq=256, tk=1024, tiles=2, ahead=1):
    L = k.shape[0]
    gw = GQA_GROUP * HEAD_DIM
    nk = L // tk
    nq = L // tq
    k_tile = lambda f: pl.BlockSpec((tk, HEAD_DIM), lambda g, i, j: (f(j), g))
    k_first = [k_tile(functools.partial(lambda j, r: r, r=r)) for r in range(ahead)]
    k_next = [k_tile(functools.partial(lambda j, r: lax.rem(tiles * j + r + ahead, nk), r=r)) for r in range(tiles)]
    return pl.pallas_call(
        functools.partial(_attn_kernel, tiles=tiles, ahead=ahead), grid=(N_KV_HEADS, nq, nk // tiles),
        in_specs=[pl.BlockSpec((gw, tq), lambda g, i, j: (g, i)),
                  pl.BlockSpec((gw, tq), lambda g, i, j: (g, jnp.minimum(i + 1, nq - 1)))]
        + k_first + k_next + [pl.BlockSpec((HEAD_DIM, tiles * tk), lambda g, i, j: (g, j))],
        out_specs=pl.BlockSpec((tq, gw), lambda g, i, j: (i, g)),
        out_shape=jax.ShapeDtypeStruct((L, ATT_WIDTH), BF16),
        scratch_shapes=[pltpu.VMEM((GQA_GROUP, 1, tq), F32)] * 2 + [pltpu.VMEM((GQA_GROUP, HEAD_DIM, tq), F32)]
        + [pltpu.VMEM((GQA_GROUP, tk, tq), F32)] * tiles + [pltpu.VMEM((GQA_GROUP, tk, tq), BF16)] * tiles,
        compiler_params=_params(3), name="attention")(qt, qt, *([k] * (ahead + tiles)), vt)


HALO = BF16_SUBLANES


def _halo_specs(tm, K, L):
    hb = tm // HALO
    n_h = L // HALO
    return [pl.BlockSpec((tm, K), lambda i, j: (i, 0)),
            pl.BlockSpec((HALO, K), lambda i, j: (jnp.maximum(i * hb - 1, 0), 0)),
            pl.BlockSpec((HALO, K), lambda i, j: (jnp.minimum((i + 1) * hb, n_h - 1), 0))]


def _fill_extended(xe_ref, x_ref, xp_ref, xn_ref):
    tm = x_ref.shape[0]

    @pl.when(pl.program_id(1) == 0)
    def _():
        xe_ref[0:HALO, :] = xp_ref[...]
        xe_ref[HALO:HALO + tm, :] = x_ref[...]
        xe_ref[HALO + tm:, :] = xn_ref[...]


def _conv3_extended(u, w_ref, b_ref):
    n_ext = u.shape[0]
    tm = n_ext - 2 * HALO
    i = pl.program_id(0)
    top = u[:HALO] * jnp.where(i > 0, 1.0, 0.0)
    bot = u[HALO + tm:] * jnp.where(i < pl.num_programs(0) - 1, 1.0, 0.0)
    u = jnp.concatenate([top, u[HALO:HALO + tm], bot], axis=0)
    um = pltpu.roll(u, 1, 0)[HALO:HALO + tm]
    up = pltpu.roll(u, n_ext - 1, 0)[HALO:HALO + tm]
    return w_ref[0:1, :] * um + w_ref[1:2, :] * u[HALO:HALO + tm] + w_ref[2:3, :] * up + b_ref[...]


def _inproj_hy_kernel(x_ref, xp_ref, xn_ref, w0_ref, w1_ref, w2_ref, cw0_ref, cw1_ref, cw2_ref,
                      cb0_ref, cb1_ref, cb2_ref, z_ref, x0_ref, xe_ref):
    _fill_extended(xe_ref, x_ref, xp_ref, xn_ref)
    xe = xe_ref[...]
    x0_ref[...] = _conv3_extended(_dot(xe, w0_ref[...]), cw0_ref, cb0_ref).astype(BF16)
    x1 = _conv3_extended(_dot(xe, w1_ref[...]), cw1_ref, cb1_ref)
    v = _conv3_extended(_dot(xe, w2_ref[...]), cw2_ref, cb2_ref)
    z_ref[...] = (x1 * v).astype(BF16)


def _inproj_hy(xb, w_in_b, l, conv_w, conv_b, tm=1024, tn=512):
    L, K = xb.shape
    C = HYENA_WIDTH
    nj = C // tn
    w_specs = [pl.BlockSpec((None, K, tn), functools.partial(lambda i, j, g: (l, 0, (ATT_IN + g * C) // tn + j), g=g))
               for g in range(3)]
    cw_specs = [pl.BlockSpec((3, tn), functools.partial(lambda i, j, g: (0, g * nj + j), g=g)) for g in range(3)]
    cb_specs = [pl.BlockSpec((1, tn), functools.partial(lambda i, j, g: (0, g * nj + j), g=g)) for g in range(3)]
    out = pl.BlockSpec((tm, tn), lambda i, j: (i, j))
    return pl.pallas_call(
        _inproj_hy_kernel, grid=(L // tm, nj),
        in_specs=_halo_specs(tm, K, L) + w_specs + cw_specs + cb_specs,
        out_specs=[out, out], out_shape=[jax.ShapeDtypeStruct((L, C), BF16)] * 2,
        scratch_shapes=[pltpu.VMEM((tm + 2 * HALO, K), BF16)],
        compiler_params=_params(2), name="inproj_hy")(
            xb, xb, xb, w_in_b, w_in_b, w_in_b, conv_w, conv_w, conv_w, conv_b, conv_b, conv_b)


def _filt_feat_kernel(fb_ref, w1t_ref, w1c_ref, w1s_ref, b1_ref, f1_ref, w2_ref, b2_ref, f2_ref,
                      gf_ref, gr_ref, *, tm):
    pos = lax.broadcasted_iota(jnp.int32, (1, tm), 1) + pl.program_id(0) * tm

    def feats(p):
        pf = p.astype(F32)
        ang = fb_ref[...] * ((2.0 * math.pi * pf) / SEQ)
        pre = (w1t_ref[...] * (pf / (SEQ - 1)) + _dot_hi(w1c_ref[...], jnp.cos(ang))
               + _dot_hi(w1s_ref[...], -jnp.sin(ang)) + b1_ref[...])
        h = jnp.sin(f1_ref[...] * pre)
        return jnp.sin(f2_ref[...] * (_dot_hi(w2_ref[...], h) + b2_ref[...]))

    gf_ref[...] = feats(pos)
    gr_ref[...] = feats(jnp.where(pos == 0, 0, SEQ - pos))


def _filt_feat(fband, w1, b1, f1, w2, b2, f2, tm=1024):
    H = FILTER_HIDDEN
    full = lambda a: pl.BlockSpec(a.shape, lambda i: (0,) * a.ndim)
    col = lambda v: v[:, None]
    args = [col(fband), col(w1[0]), w1[1:1 + FILTER_BANDS].T, w1[1 + FILTER_BANDS:].T, col(b1), col(f1),
            w2.T, col(b2), col(f2)]
    out = pl.BlockSpec((H, tm), lambda i: (0, i))
    return pl.pallas_call(
        functools.partial(_filt_feat_kernel, tm=tm), grid=(SEQ // tm,),
        in_specs=[full(a) for a in args], out_specs=[out, out],
        out_shape=[jax.ShapeDtypeStruct((H, SEQ), F32)] * 2,
        compiler_params=_params(1), name="filt_feat")(*args)


def _dot_t(a_t, b):
    return lax.dot_general(a_t.astype(BF16), b.astype(BF16), (((0,), (0,)), ((), ())), preferred_element_type=F32)


def _filt_gen_kernel(gf_ref, gr_ref, w3f_ref, w3b_ref, b3f_ref, b3b_ref, df_ref, db_ref,
                     kf_ref, kb_ref, norm_ref, *, tm):
    i = pl.program_id(0)
    pos = lax.broadcasted_iota(jnp.int32, (tm, HYENA_WIDTH), 0) + i * tm
    pos_r = jnp.where(pos == 0, 0, SEQ - pos)
    t = pos.astype(F32) / (SEQ - 1)
    tr = pos_r.astype(F32) / (SEQ - 1)
    hf = (_dot_t(gf_ref[...], w3f_ref[...]) + b3f_ref[...]) * jnp.exp(-t * jnp.abs(df_ref[...]))
    hb = (_dot_t(gr_ref[...], w3b_ref[...]) + b3b_ref[...]) * jnp.exp(-tr * jnp.abs(db_ref[...]))

    @pl.when(i == 0)
    def _():
        norm_ref[...] = jnp.zeros(norm_ref.shape, F32)

    norm_ref[...] += jnp.sum(jnp.abs(hf) + jnp.abs(hb), axis=0, keepdims=True)
    kf_ref[...] = hf.astype(BF16)
    kb_ref[...] = jnp.where(pos == 0, 0.0, -hb).astype(BF16)


def _filt_gen(gf, gr, w3, b3, decay, tm=512):
    C, H = HYENA_WIDTH, FILTER_HIDDEN
    g = pl.BlockSpec((H, tm), lambda i: (0, i))
    w = pl.BlockSpec((H, C), lambda i: (0, 0))
    vec = pl.BlockSpec((1, C), lambda i: (0, 0))
    out = pl.BlockSpec((tm, C), lambda i: (i, 0))
    return pl.pallas_call(
        functools.partial(_filt_gen_kernel, tm=tm), grid=(SEQ // tm,),
        in_specs=[g, g, w, w, vec, vec, vec, vec], out_specs=[out, out, vec],
        out_shape=[jax.ShapeDtypeStruct((SEQ, C), BF16)] * 2 + [jax.ShapeDtypeStruct((1, C), F32)],
        compiler_params=_params(1), name="filt_gen")(
            gf, gr, w3[:, :C], w3[:, C:], b3[None, :C], b3[None, C:], decay[0:1], decay[1:2])


def _fft_tables():
    N = FFT_N
    n2 = np.arange(FFT_R, dtype=np.int64)[:, None, None]
    k1 = np.arange(FFT_R // 2, dtype=np.int64)[None, :, None]
    n1 = np.arange(FFT_R, dtype=np.int64)[None, None, :]
    theta = (((FFT_R * n1 + n2) * (2 * k1 + 1)) % (2 * N)) * (math.pi / N)
    c, s = np.cos(theta), np.sin(theta)
    ta = np.concatenate([c, -s], axis=1)
    ti = np.transpose(ta[:, :, :FFT_R // 2], (0, 2, 1)) * (2.0 / N)
    j = np.arange(FFT_R, dtype=np.int64)
    phi = ((j[:, None] * j[None, :]) % FFT_R) * (2.0 * math.pi / FFT_R)
    cm, sm = np.cos(phi), np.sin(phi)
    f2f = np.block([[cm, sm], [-sm, cm]])
    f2i = np.block([[cm, -sm], [sm, cm]])
    return tuple(jnp.asarray(t.astype(np.float32)).astype(BF16) for t in (ta, ti, f2f, f2i))


def _fft_conv_kernel(z_ref, kf_ref, kb_ref, ta_ref, ti_ref, f2f_ref, f2i_ref, y_ref, pz_ref, pk_ref, s_ref):
    half = FFT_R // 2
    tc = z_ref.shape[1]
    zeros_hi = jnp.zeros((half, tc), BF16)

    for c in range(FFT_R // N2_CHUNK):
        off = c * N2_CHUNK
        for n1 in range(half):
            pz_ref[pl.ds(n1 * P_PITCH, N2_CHUNK), :] = z_ref[pl.ds(n1 * FFT_R + off, N2_CHUNK), :].astype(F32)
            pk_ref[pl.ds(n1 * P_PITCH, N2_CHUNK), :] = kf_ref[pl.ds(n1 * FFT_R + off, N2_CHUNK), :].astype(F32)
            pk_ref[pl.ds((half + n1) * P_PITCH, N2_CHUNK), :] = (
                kb_ref[pl.ds(n1 * FFT_R + off, N2_CHUNK), :].astype(F32))

        def stage_a(j, carry):
            n2 = off + j
            xz = pz_ref[pl.ds(j, half, stride=P_PITCH), :].astype(BF16)
            xk = pk_ref[pl.ds(j, FFT_R, stride=P_PITCH), :].astype(BF16)
            rhs = jnp.concatenate([jnp.concatenate([xz, zeros_hi], axis=0), xk], axis=1)
            a = _dot(ta_ref[n2], rhs)
            row = pl.multiple_of(n2 * S_PITCH, 8)
            s_ref[0, pl.ds(row, FFT_R), :] = a[:, :tc]
            s_ref[1, pl.ds(row, FFT_R), :] = a[:, tc:]
            return carry

        lax.fori_loop(0, N2_CHUNK, stage_a, 0, unroll=FFT_UNROLL)

    def rows(k1):
        return pl.ds(k1, FFT_R, stride=S_PITCH), pl.ds(half + k1, FFT_R, stride=S_PITCH)

    def spectrum(k1):
        re_rows, im_rows = rows(k1)
        rhs = jnp.concatenate(
            [jnp.concatenate([s_ref[0, re_rows, :], s_ref[1, re_rows, :]], axis=1),
             jnp.concatenate([s_ref[0, im_rows, :], s_ref[1, im_rows, :]], axis=1)], axis=0).astype(BF16)
        return _dot(f2f_ref[...], rhs)

    def product(x):
        zr, zi = x[:FFT_R, :tc], x[FFT_R:, :tc]
        kr, ki = x[:FFT_R, tc:], x[FFT_R:, tc:]
        return jnp.concatenate([zr * kr - zi * ki, zr * ki + zi * kr], axis=0).astype(BF16)

    def stage_b(i, carry):
        k1s = [i * FFT_KB + u for u in range(FFT_KB)]
        xs = [spectrum(k1) for k1 in k1s]
        for u in range(0, FFT_KB, 2):
            d = _dot(f2i_ref[...], jnp.concatenate([product(xs[u]), product(xs[u + 1])], axis=1))
            for v in range(2):
                re_rows, im_rows = rows(k1s[u + v])
                s_ref[0, re_rows, :] = d[:FFT_R, v * tc:(v + 1) * tc]
                s_ref[0, im_rows, :] = d[FFT_R:, v * tc:(v + 1) * tc]
        return carry

    lax.fori_loop(0, half // FFT_KB, stage_b, 0)

    def stage_out(n2, carry):
        d = s_ref[0, pl.ds(pl.multiple_of(n2 * S_PITCH, 8), FFT_R), :].astype(BF16)
        y_ref[pl.ds(n2, half, stride=FFT_R), :] = _dot(ti_ref[n2], d)
        return carry

    lax.fori_loop(0, FFT_R, stage_out, 0, unroll=FFT_UNROLL)


def _fft_conv(z, kf, kb, tables):
    L, C = z.shape
    tc = FFT_TC
    ta, ti, f2f, f2i = tables
    col = pl.BlockSpec((L, tc), lambda j: (0, j))
    full = lambda a: pl.BlockSpec(a.shape, lambda j: (0,) * a.ndim)
    half = FFT_R // 2
    return pl.pallas_call(
        _fft_conv_kernel, grid=(C // tc,),
        in_specs=[col, col, col, full(ta), full(ti), full(f2f), full(f2i)],
        out_specs=col, out_shape=jax.ShapeDtypeStruct((L, C), F32),
        scratch_shapes=[pltpu.VMEM((half * P_PITCH, tc), F32), pltpu.VMEM((FFT_R * P_PITCH, tc), F32),
                        pltpu.VMEM((2, FFT_R * S_PITCH, tc), F32)],
        compiler_params=_params(1), name="fft_conv")(z, kf, kb, ta, ti, f2f, f2i)


def _outproj_kernel(a_ref, x0_ref, z_ref, yc_ref, norm_ref, skip_ref, ag_ref, hg_ref,
                    w_ref, b_ref, x_ref, g_ref, beta_ref, xf_ref, xb_ref, *, sub_rows):
    inv_norm = 1.0 / norm_ref[...]
    for r in range(a_ref.shape[0] // sub_rows):
        rows = pl.ds(r * sub_rows, sub_rows)
        a = _rms_norm(a_ref[rows, :].astype(F32), ag_ref[...])
        z = z_ref[rows, :].astype(F32)
        hy = x0_ref[rows, :].astype(F32) * (yc_ref[rows, :] * inv_norm + skip_ref[...] * z)
        h = _rms_norm(hy, hg_ref[...])
        mix = _dot(jnp.concatenate([a.astype(BF16), h.astype(BF16)], axis=1), w_ref[...]) + b_ref[...]
        y = _layer_norm(ALPHA * x_ref[rows, :] + mix, g_ref[...], beta_ref[...])
        xf_ref[rows, :] = y
        xb_ref[rows, :] = y.astype(BF16)


def _outproj(a, x0, z, yc, norm, skip, ag, hg, w_out, l, b_out, x, g, beta, tm=512, sub_rows=128):
    L, D = x.shape
    C = HYENA_WIDTH
    half = pl.BlockSpec((tm, C), lambda i: (i, 0))
    hvec = pl.BlockSpec((1, C), lambda i: (0, 0))
    row = pl.BlockSpec((tm, D), lambda i: (i, 0))
    vec = pl.BlockSpec((1, D), lambda i: (0, 0))
    w = pl.BlockSpec((None, ATT_WIDTH + C, D), lambda i: (l, 0, 0), pipeline_mode=pl.Buffered(1))
    return pl.pallas_call(
        functools.partial(_outproj_kernel, sub_rows=sub_rows), grid=(L // tm,),
        in_specs=[half, half, half, half, hvec, hvec, hvec, hvec, w, vec, row, vec, vec],
        out_specs=[row, row],
        out_shape=[jax.ShapeDtypeStruct((L, D), F32), jax.ShapeDtypeStruct((L, D), BF16)],
        compiler_params=_params(1), name="outproj_ln")(
            a, x0, z, yc, norm, skip, ag, hg, w_out, b_out, x, g, beta)


def _gelu_tanh(x):
    return x * (0.5 * (1.0 + jnp.tanh(math.sqrt(2.0 / math.pi) * (x + 0.044715 * (x * x * x)))))


def _ffn_up_kernel(x_ref, xp_ref, xn_ref, wg_ref, wv_ref, bg_ref, bv_ref, cw_ref, cb_ref, h_ref, xe_ref):
    _fill_extended(xe_ref, x_ref, xp_ref, xn_ref)
    gate = _conv3_extended(_dot(xe_ref[...], wg_ref[...]) + bg_ref[...], cw_ref, cb_ref)
    val = _dot(x_ref[...], wv_ref[...]) + bv_ref[...]
    h_ref[...] = (_gelu_tanh(gate) * val).astype(BF16)


def _ffn_up(xb, w_gate, w_val, b_gate, b_val, conv_w, conv_b, l, tm=1024, tn=512):
    L, K = xb.shape
    w = pl.BlockSpec((None, K, tn), lambda i, j: (l, 0, j))
    vec = pl.BlockSpec((1, tn), lambda i, j: (0, j))
    return pl.pallas_call(
        _ffn_up_kernel, grid=(L // tm, pl.cdiv(D_FF, tn)),
        in_specs=_halo_specs(tm, K, L) + [w, w, vec, vec, pl.BlockSpec((3, tn), lambda i, j: (0, j)), vec],
        out_specs=pl.BlockSpec((tm, tn), lambda i, j: (i, j)),
        out_shape=jax.ShapeDtypeStruct((L, D_FF), BF16),
        scratch_shapes=[pltpu.VMEM((tm + 2 * HALO, K), BF16)],
        compiler_params=_params(2), name="ffn_up")(xb, xb, xb, w_gate, w_val, b_gate, b_val, conv_w, conv_b)


def _ffn_down_kernel(h_ref, w_ref, b_ref, x_ref, g_ref, beta_ref, xf_ref, xb_ref):
    ffn = _dot(h_ref[...], w_ref[...]) + b_ref[...]
    y = _layer_norm(ALPHA * x_ref[...] + ffn, g_ref[...], beta_ref[...])
    xf_ref[...] = y
    xb_ref[...] = y.astype(BF16)


def _ffn_down(h, w_down, b_down, x, g, beta, l, tm=256):
    L, D = x.shape
    row = pl.BlockSpec((tm, D), lambda i: (i, 0))
    vec = pl.BlockSpec((1, D), lambda i: (0, 0))
    return pl.pallas_call(
        _ffn_down_kernel, grid=(L // tm,),
        in_specs=[pl.BlockSpec((tm, D_FF), lambda i: (i, 0)),
                  pl.BlockSpec((None, D_FF, D), lambda i: (l, 0, 0), pipeline_mode=pl.Buffered(1)),
                  vec, row, vec, vec],
        out_specs=[row, row],
        out_shape=[jax.ShapeDtypeStruct((L, D), F32), jax.ShapeDtypeStruct((L, D), BF16)],
        compiler_params=_params(1), name="ffn_down_ln")(h, w_down, b_down, x, g, beta)


def _rope_tables(L):
    rows = L // GRID_W
    row_pos = np.repeat(np.arange(rows, dtype=np.float64), GRID_W)
    col_pos = np.tile(np.arange(GRID_W, dtype=np.float64), rows)
    dims = HEAD_DIM // 2

    def axis_table(pos):
        inv = ROPE_THETA ** (-np.arange(0, dims, 2, dtype=np.float64) / dims)
        ang = pos[:, None] * inv[None, :]
        ang = np.concatenate([ang, ang], axis=-1)
        return np.cos(ang), np.sin(ang)

    cos_r, sin_r = axis_table(row_pos)
    cos_c, sin_c = axis_table(col_pos)
    cos = np.concatenate([cos_r, cos_c], axis=-1)
    sin = np.concatenate([sin_r, sin_c], axis=-1)
    first = (np.arange(HEAD_DIM) % dims) < dims // 2
    tabs = (cos, np.where(first, -sin, 0.0), np.where(first, 0.0, sin))
    return tuple(jnp.asarray(np.ascontiguousarray(t.T.astype(np.float32))) for t in tabs)


def kernel(x, ln_in_g, ln_in_b, w_in, q_norm_g, k_norm_g, hy_conv_w, hy_conv_b, filt_w1, filt_b1, filt_f1, filt_w2, filt_b2, filt_f2, filt_w3, filt_b3, filt_decay, hy_skip, att_out_g, hy_out_g, w_out, b_out, ln1_g, ln1_b, w_up, b_up, ffn_conv_w, ffn_conv_b, w_down, b_down, ln2_g, ln2_b):
    assert x.shape == (1, SEQ, D_MODEL) and w_in.shape == (DEPTH, D_MODEL, ATT_IN + HY_IN)
    rope_t = _rope_tables(SEQ)
    tables = _fft_tables()
    fband = jnp.linspace(1e-4, FILTER_BANDS - 1, FILTER_BANDS, dtype=F32)

    w_in_b = w_in.astype(BF16)
    w_out_b = w_out.astype(BF16)
    w_up_b = w_up.astype(BF16)
    w_val_b = w_up_b[:, :, D_FF:]
    w_down_b = w_down.astype(BF16)

    xf, xb = _input_ln(x[0], ln_in_g[None], ln_in_b[None])
    for l in range(DEPTH):
        qt, k, vt = _inproj_att(xb, w_in_b, l, q_norm_g[l], k_norm_g[l], rope_t)
        att = _attention(qt, k, vt)
        z, x0 = _inproj_hy(xb, w_in_b, l, hy_conv_w[l], hy_conv_b[l][None])
        gf, gr = _filt_feat(fband, filt_w1[l], filt_b1[l], filt_f1[l], filt_w2[l], filt_b2[l], filt_f2[l])
        kf, kb, norm = _filt_gen(gf, gr, filt_w3[l], filt_b3[l], filt_decay[l])
        yc = _fft_conv(z, kf, kb, tables)
        xf, xb = _outproj(att, x0, z, yc, norm, hy_skip[l][None], att_out_g[l][None], hy_out_g[l][None],
                          w_out_b, l, b_out[l][None], xf, ln1_g[l][None], ln1_b[l][None])
        h = _ffn_up(xb, w_up_b, w_val_b, b_up[l][None, :D_FF], b_up[l][None, D_FF:],
                    ffn_conv_w[l], ffn_conv_b[l][None], l)
        xf, xb = _ffn_down(h, w_down_b, b_down[l][None], xf, ln2_g[l][None], ln2_b[l][None], l)
    return xf[None]
```

```python
import functools
import math

import jax
import jax.numpy as jnp
import numpy as np
from jax import lax
from jax.experimental import pallas as pl
from jax.experimental.pallas import tpu as pltpu

F32 = jnp.float32
BF16 = jnp.bfloat16

D_MODEL = 2048
SEQ = 8192
DEPTH = 2
ATT_WIDTH = 1024
HYENA_WIDTH = 1024
HEAD_DIM = 128
N_KV_HEADS = 2
GQA_GROUP = 4
KV_WIDTH = N_KV_HEADS * HEAD_DIM
ATT_IN = ATT_WIDTH + 2 * KV_WIDTH
HY_IN = 3 * HYENA_WIDTH
FILTER_HIDDEN = 64
FILTER_BANDS = 16
D_FF = 5504
GRID_W = 64
ROPE_THETA = 10000.0
ALPHA = (2.0 * DEPTH) ** 0.25
LN_EPS = 1e-5
RMS_EPS = 1e-6

LANES = 128
BF16_SUBLANES = 16
VMEM_LIMIT = 56 * 1024 * 1024
ATT_CHUNK = 128

FFT_R = 128
FFT_N = FFT_R * FFT_R
FFT_TC = 128
S_PITCH = 136
N2_CHUNK = 32
P_PITCH = 40
FFT_UNROLL = 16
FFT_KB = 16

_dot = functools.partial(jnp.dot, preferred_element_type=F32)
_dot_hi = functools.partial(jnp.dot, preferred_element_type=F32, precision=lax.Precision.HIGHEST)


def _params(n_axes, flags=None):
    return pltpu.CompilerParams(dimension_semantics=("arbitrary",) * n_axes,
                                vmem_limit_bytes=VMEM_LIMIT, flags=flags)


def _layer_norm(v, g, b):
    mu = jnp.mean(v, axis=-1, keepdims=True)
    d = v - mu
    var = jnp.mean(d * d, axis=-1, keepdims=True)
    return d * lax.rsqrt(var + LN_EPS) * g + b


def _rms_norm(v, g):
    ms = jnp.mean(v * v, axis=-1, keepdims=True)
    return v * lax.rsqrt(ms + RMS_EPS) * g


def _ln_kernel(x_ref, g_ref, b_ref, xf_ref, xb_ref):
    y = _layer_norm(x_ref[...], g_ref[...], b_ref[...])
    xf_ref[...] = y
    xb_ref[...] = y.astype(BF16)


def _input_ln(x, g, b, tm=512):
    L, D = x.shape
    row = pl.BlockSpec((tm, D), lambda i: (i, 0))
    vec = pl.BlockSpec((1, D), lambda i: (0, 0))
    return pl.pallas_call(
        _ln_kernel, grid=(L // tm,), in_specs=[row, vec, vec], out_specs=[row, row],
        out_shape=[jax.ShapeDtypeStruct((L, D), F32), jax.ShapeDtypeStruct((L, D), BF16)],
        compiler_params=_params(1), name="input_ln")(x, g, b)


def _inproj_head_kernel(x_ref, w_ref, g_ref, cos_ref, sa_ref, sb_ref, o_ref, *, scale, transposed_out):
    acc = _dot(x_ref[...], w_ref[...])
    tm = acc.shape[0]
    gain = jnp.tile(g_ref[...], (1, tm // LANES))
    for h in range(w_ref.shape[1] // HEAD_DIM):
        sl = slice(h * HEAD_DIM, (h + 1) * HEAD_DIM)
        xt = acc[:, sl].T
        hn = xt * lax.rsqrt(jnp.mean(xt * xt, axis=0, keepdims=True) + RMS_EPS) * gain
        rot = pltpu.roll(hn, HEAD_DIM - 32, 0) * sa_ref[...] + pltpu.roll(hn, 32, 0) * sb_ref[...]
        r = hn * cos_ref[...] + rot
        if transposed_out:
            o_ref[sl, :] = (r * scale).astype(BF16)
        else:
            o_ref[:, sl] = r.T.astype(BF16)


def _inproj_vt_kernel(x_ref, w_ref, o_ref):
    o_ref[...] = _dot(x_ref[...], w_ref[...]).T.astype(BF16)


def _inproj_att(xb, w_in_b, l, qg, kg, tabs_t, tm=1024):
    L, K = xb.shape
    tn = KV_WIDTH
    tq = 2 * tn
    n_q = ATT_WIDTH // tn
    x_spec = pl.BlockSpec((tm, K), lambda i, j: (i, 0))
    gain = pl.BlockSpec((HEAD_DIM, LANES), lambda i, j: (0, 0))
    gain_col = lambda g: jnp.broadcast_to(g.reshape(HEAD_DIM, 1), (HEAD_DIM, LANES))
    tab_t = pl.BlockSpec((HEAD_DIM, tm), lambda i, j: (0, i))
    q_scale = HEAD_DIM ** -0.5 * math.log2(math.e)
    qt = pl.pallas_call(
        functools.partial(_inproj_head_kernel, scale=q_scale, transposed_out=True), grid=(L // tm, ATT_WIDTH // tq),
        in_specs=[x_spec, pl.BlockSpec((None, K, tq), lambda i, j: (l, 0, j)), gain, tab_t, tab_t, tab_t],
        out_specs=pl.BlockSpec((tq, tm), lambda i, j: (j, i)),
        out_shape=jax.ShapeDtypeStruct((ATT_WIDTH, L), BF16),
        compiler_params=_params(2), name="inproj_q")(xb, w_in_b, gain_col(qg), *tabs_t)
    k = pl.pallas_call(
        functools.partial(_inproj_head_kernel, scale=1.0, transposed_out=False), grid=(L // tm, 1),
        in_specs=[x_spec, pl.BlockSpec((None, K, tn), lambda i, j: (l, 0, n_q)), gain, tab_t, tab_t, tab_t],
        out_specs=pl.BlockSpec((tm, tn), lambda i, j: (i, 0)),
        out_shape=jax.ShapeDtypeStruct((L, KV_WIDTH), BF16),
        compiler_params=_params(2), name="inproj_k")(xb, w_in_b, gain_col(kg), *tabs_t)
    vt = pl.pallas_call(
        _inproj_vt_kernel, grid=(L // tm, 1),
        in_specs=[x_spec, pl.BlockSpec((None, K, tn), lambda i, j: (l, 0, n_q + 1))],
        out_specs=pl.BlockSpec((tn, tm), lambda i, j: (0, i)),
        out_shape=jax.ShapeDtypeStruct((KV_WIDTH, L), BF16),
        compiler_params=_params(2), name="inproj_v")(xb, w_in_b)
    return qt, k, vt


def _attn_kernel(qt_ref, qtn_ref, *refs, tiles, ahead):
    k_first, k_next = refs[:ahead], refs[ahead:ahead + tiles]
    vt_ref, o_ref, m_ref, l_ref, acc_ref = refs[ahead + tiles:ahead + tiles + 5]
    s_refs = refs[ahead + tiles + 5:ahead + 2 * tiles + 5]
    p_refs = refs[ahead + 2 * tiles + 5:]
    i = pl.program_id(1)
    j = pl.program_id(2)
    last = j == pl.num_programs(2) - 1
    tk, tq = s_refs[0].shape[1:]
    sub = 8
    chunks = [pl.ds(c * ATT_CHUNK, ATT_CHUNK) for c in range(tk // ATT_CHUNK)]
    heads = range(GQA_GROUP)

    def scores(k_ref, s_ref, qt):
        k = k_ref[...]
        for h in heads:
            s_ref[h] = _dot(k, qt[h * HEAD_DIM:(h + 1) * HEAD_DIM, :])

    def softmax_pv(s_ref, p_ref, vt):
        for h in heads:
            m_part = jnp.full((sub, tq), -jnp.inf, F32)
            for rows in chunks:
                m_part = jnp.maximum(m_part, jnp.max(s_ref[h, rows, :].reshape(ATT_CHUNK // sub, sub, tq), axis=0))
            m_prev = m_ref[h]
            m_new = jnp.maximum(m_prev, jnp.max(m_part, axis=0, keepdims=True))
            alpha = jnp.exp2(m_prev - m_new)
            l_part = jnp.zeros((sub, tq), F32)
            for rows in chunks:
                p = jnp.exp2(s_ref[h, rows, :] - m_new)
                l_part = l_part + jnp.sum(p.reshape(ATT_CHUNK // sub, sub, tq), axis=0)
                p_ref[h, rows, :] = p.astype(BF16)
            l_ref[h] = alpha * l_ref[h] + jnp.sum(l_part, axis=0, keepdims=True)
            acc_ref[h] = alpha * acc_ref[h] + _dot(vt, p_ref[h])
            m_ref[h] = m_new

    @pl.when(j == 0)
    def _():
        m_ref[...] = jnp.full(m_ref.shape, -jnp.inf, F32)
        l_ref[...] = jnp.zeros(l_ref.shape, F32)
        acc_ref[...] = jnp.zeros(acc_ref.shape, F32)

    @pl.when(jnp.logical_and(j == 0, i == 0))
    def _():
        for r in range(ahead):
            scores(k_first[r], s_refs[r], qt_ref[...])

    qt = qt_ref[...]
    qt_next = jnp.where(last, qtn_ref[...], qt)
    for r in range(tiles):
        scores(k_next[r], s_refs[(r + ahead) % tiles], qt if r + ahead < tiles else qt_next)
        softmax_pv(s_refs[r], p_refs[r], vt_ref[:, r * tk:(r + 1) * tk])

    @pl.when(last)
    def _():
        for h in heads:
            o_ref[:, h * HEAD_DIM:(h + 1) * HEAD_DIM] = (acc_ref[h] / l_ref[h]).T.astype(BF16)


def _attention(qt, k, vt, tq=256, tk=1024, tiles=2, ahead=1):
    L = k.shape[0]
    gw = GQA_GROUP * HEAD_DIM
    nk = L // tk
    nq = L // tq
    k_tile = lambda f: pl.BlockSpec((tk, HEAD_DIM), lambda g, i, j: (f(j), g))
    k_first = [k_tile(functools.partial(lambda j, r: r, r=r)) for r in range(ahead)]
    k_next = [k_tile(functools.partial(lambda j, r: lax.rem(tiles * j + r + ahead, nk), r=r)) for r in range(tiles)]
    return pl.pallas_call(
        functools.partial(_attn_kernel, tiles=tiles, ahead=ahead), grid=(N_KV_HEADS, nq, nk // tiles),
        in_specs=[pl.BlockSpec((gw, tq), lambda g, i, j: (g, i)),
                  pl.BlockSpec((gw, tq), lambda g, i, j: (g, jnp.minimum(i + 1, nq - 1)))]
        + k_first + k_next + [pl.BlockSpec((HEAD_DIM, tiles * tk), lambda g, i, j: (g, j))],
        out_specs=pl.BlockSpec((tq, gw), lambda g, i, j: (i, g)),
        out_shape=jax.ShapeDtypeStruct((L, ATT_WIDTH), BF16),
        scratch_shapes=[pltpu.VMEM((GQA_GROUP, 1, tq), F32)] * 2 + [pltpu.VMEM((GQA_GROUP, HEAD_DIM, tq), F32)]
        + [pltpu.VMEM((GQA_GROUP, tk, tq), F32)] * tiles + [pltpu.VMEM((GQA_GROUP, tk, tq), BF16)] * tiles,
        compiler_params=_params(3), name="attention")(qt, qt, *([k] * (ahead + tiles)), vt)


HALO = BF16_SUBLANES


def _halo_specs(tm, K, L):
    hb = tm // HALO
    n_h = L // HALO
    return [pl.BlockSpec((tm, K), lambda i, j: (i, 0)),
            pl.BlockSpec((HALO, K), lambda i, j: (jnp.maximum(i * hb - 1, 0), 0)),
            pl.BlockSpec((HALO, K), lambda i, j: (jnp.minimum((i + 1) * hb, n_h - 1), 0))]


def _fill_extended(xe_ref, x_ref, xp_ref, xn_ref):
    tm = x_ref.shape[0]

    @pl.when(pl.program_id(1) == 0)
    def _():
        xe_ref[0:HALO, :] = xp_ref[...]
        xe_ref[HALO:HALO + tm, :] = x_ref[...]
        xe_ref[HALO + tm:, :] = xn_ref[...]


def _conv3_extended(u, w_ref, b_ref):
    n_ext = u.shape[0]
    tm = n_ext - 2 * HALO
    i = pl.program_id(0)
    top = u[:HALO] * jnp.where(i > 0, 1.0, 0.0)
    bot = u[HALO + tm:] * jnp.where(i < pl.num_programs(0) - 1, 1.0, 0.0)
    u = jnp.concatenate([top, u[HALO:HALO + tm], bot], axis=0)
    um = pltpu.roll(u, 1, 0)[HALO:HALO + tm]
    up = pltpu.roll(u, n_ext - 1, 0)[HALO:HALO + tm]
    return w_ref[0:1, :] * um + w_ref[1:2, :] * u[HALO:HALO + tm] + w_ref[2:3, :] * up + b_ref[...]


def _inproj_hy_kernel(x_ref, xp_ref, xn_ref, w0_ref, w1_ref, w2_ref, cw0_ref, cw1_ref, cw2_ref,
                      cb0_ref, cb1_ref, cb2_ref, z_ref, x0_ref, xe_ref):
    _fill_extended(xe_ref, x_ref, xp_ref, xn_ref)
    xe = xe_ref[...]
    x0_ref[...] = _conv3_extended(_dot(xe, w0_ref[...]), cw0_ref, cb0_ref).astype(BF16)
    x1 = _conv3_extended(_dot(xe, w1_ref[...]), cw1_ref, cb1_ref)
    v = _conv3_extended(_dot(xe, w2_ref[...]), cw2_ref, cb2_ref)
    z_ref[...] = (x1 * v).astype(BF16)


def _inproj_hy(xb, w_in_b, l, conv_w, conv_b, tm=1024, tn=512):
    L, K = xb.shape
    C = HYENA_WIDTH
    nj = C // tn
    w_specs = [pl.BlockSpec((None, K, tn), functools.partial(lambda i, j, g: (l, 0, (ATT_IN + g * C) // tn + j), g=g))
               for g in range(3)]
    cw_specs = [pl.BlockSpec((3, tn), functools.partial(lambda i, j, g: (0, g * nj + j), g=g)) for g in range(3)]
    cb_specs = [pl.BlockSpec((1, tn), functools.partial(lambda i, j, g: (0, g * nj + j), g=g)) for g in range(3)]
    out = pl.BlockSpec((tm, tn), lambda i, j: (i, j))
    return pl.pallas_call(
        _inproj_hy_kernel, grid=(L // tm, nj),
        in_specs=_halo_specs(tm, K, L) + w_specs + cw_specs + cb_specs,
        out_specs=[out, out], out_shape=[jax.ShapeDtypeStruct((L, C), BF16)] * 2,
        scratch_shapes=[pltpu.VMEM((tm + 2 * HALO, K), BF16)],
        compiler_params=_params(2), name="inproj_hy")(
            xb, xb, xb, w_in_b, w_in_b, w_in_b, conv_w, conv_w, conv_w, conv_b, conv_b, conv_b)


def _filt_feat_kernel(fb_ref, w1t_ref, w1c_ref, w1s_ref, b1_ref, f1_ref, w2_ref, b2_ref, f2_ref,
                      gf_ref, gr_ref, *, tm):
    pos = lax.broadcasted_iota(jnp.int32, (1, tm), 1) + pl.program_id(0) * tm

    def feats(p):
        pf = p.astype(F32)
        ang = fb_ref[...] * ((2.0 * math.pi * pf) / SEQ)
        pre = (w1t_ref[...] * (pf / (SEQ - 1)) + _dot_hi(w1c_ref[...], jnp.cos(ang))
               + _dot_hi(w1s_ref[...], -jnp.sin(ang)) + b1_ref[...])
        h = jnp.sin(f1_ref[...] * pre)
        return jnp.sin(f2_ref[...] * (_dot_hi(w2_ref[...], h) + b2_ref[...]))

    gf_ref[...] = feats(pos)
    gr_ref[...] = feats(jnp.where(pos == 0, 0, SEQ - pos))


def _filt_feat(fband, w1, b1, f1, w2, b2, f2, tm=1024):
    H = FILTER_HIDDEN
    full = lambda a: pl.BlockSpec(a.shape, lambda i: (0,) * a.ndim)
    col = lambda v: v[:, None]
    args = [col(fband), col(w1[0]), w1[1:1 + FILTER_BANDS].T, w1[1 + FILTER_BANDS:].T, col(b1), col(f1),
            w2.T, col(b2), col(f2)]
    out = pl.BlockSpec((H, tm), lambda i: (0, i))
    return pl.pallas_call(
        functools.partial(_filt_feat_kernel, tm=tm), grid=(SEQ // tm,),
        in_specs=[full(a) for a in args], out_specs=[out, out],
        out_shape=[jax.ShapeDtypeStruct((H, SEQ), F32)] * 2,
        compiler_params=_params(1), name="filt_feat")(*args)


def _dot_t(a_t, b):
    return lax.dot_general(a_t.astype(BF16), b.astype(BF16), (((0,), (0,)), ((), ())), preferred_element_type=F32)


def _filt_gen_kernel(gf_ref, gr_ref, w3f_ref, w3b_ref, b3f_ref, b3b_ref, df_ref, db_ref,
                     kf_ref, kb_ref, norm_ref, *, tm):
    i = pl.program_id(0)
    pos = lax.broadcasted_iota(jnp.int32, (tm, HYENA_WIDTH), 0) + i * tm
    pos_r = jnp.where(pos == 0, 0, SEQ - pos)
    t = pos.astype(F32) / (SEQ - 1)
    tr = pos_r.astype(F32) / (SEQ - 1)
    hf = (_dot_t(gf_ref[...], w3f_ref[...]) + b3f_ref[...]) * jnp.exp(-t * jnp.abs(df_ref[...]))
    hb = (_dot_t(gr_ref[...], w3b_ref[...]) + b3b_ref[...]) * jnp.exp(-tr * jnp.abs(db_ref[...]))

    @pl.when(i == 0)
    def _():
        norm_ref[...] = jnp.zeros(norm_ref.shape, F32)

    norm_ref[...] += jnp.sum(jnp.abs(hf) + jnp.abs(hb), axis=0, keepdims=True)
    kf_ref[...] = hf.astype(BF16)
    kb_ref[...] = jnp.where(pos == 0, 0.0, -hb).astype(BF16)


def _filt_gen(gf, gr, w3, b3, decay, tm=512):
    C, H = HYENA_WIDTH, FILTER_HIDDEN
    g = pl.BlockSpec((H, tm), lambda i: (0, i))
    w = pl.BlockSpec((H, C), lambda i: (0, 0))
    vec = pl.BlockSpec((1, C), lambda i: (0, 0))
    out = pl.BlockSpec((tm, C), lambda i: (i, 0))
    return pl.pallas_call(
        functools.partial(_filt_gen_kernel, tm=tm), grid=(SEQ // tm,),
        in_specs=[g, g, w, w, vec, vec, vec, vec], out_specs=[out, out, vec],
        out_shape=[jax.ShapeDtypeStruct((SEQ, C), BF16)] * 2 + [jax.ShapeDtypeStruct((1, C), F32)],
        compiler_params=_params(1), name="filt_gen")(
            gf, gr, w3[:, :C], w3[:, C:], b3[None, :C], b3[None, C:], decay[0:1], decay[1:2])


def _fft_tables():
    N = FFT_N
    n2 = np.arange(FFT_R, dtype=np.int64)[:, None, None]
    k1 = np.arange(FFT_R // 2, dtype=np.int64)[None, :, None]
    n1 = np.arange(FFT_R, dtype=np.int64)[None, None, :]
    theta = (((FFT_R * n1 + n2) * (2 * k1 + 1)) % (2 * N)) * (math.pi / N)
    c, s = np.cos(theta), np.sin(theta)
    ta = np.concatenate([c, -s], axis=1)
    ti = np.transpose(ta[:, :, :FFT_R // 2], (0, 2, 1)) * (2.0 / N)
    j = np.arange(FFT_R, dtype=np.int64)
    phi = ((j[:, None] * j[None, :]) % FFT_R) * (2.0 * math.pi / FFT_R)
    cm, sm = np.cos(phi), np.sin(phi)
    f2f = np.block([[cm, sm], [-sm, cm]])
    f2i = np.block([[cm, -sm], [sm, cm]])
    return tuple(jnp.asarray(t.astype(np.float32)).astype(BF16) for t in (ta, ti, f2f, f2i))


def _fft_conv_kernel(z_ref, kf_ref, kb_ref, ta_ref, ti_ref, f2f_ref, f2i_ref, y_ref, pz_ref, pk_ref, s_ref):
    half = FFT_R // 2
    tc = z_ref.shape[1]
    zeros_hi = jnp.zeros((half, tc), BF16)

    for c in range(FFT_R // N2_CHUNK):
        off = c * N2_CHUNK
        for n1 in range(half):
            pz_ref[pl.ds(n1 * P_PITCH, N2_CHUNK), :] = z_ref[pl.ds(n1 * FFT_R + off, N2_CHUNK), :].astype(F32)
            pk_ref[pl.ds(n1 * P_PITCH, N2_CHUNK), :] = kf_ref[pl.ds(n1 * FFT_R + off, N2_CHUNK), :].astype(F32)
            pk_ref[pl.ds((half + n1) * P_PITCH, N2_CHUNK), :] = (
                kb_ref[pl.ds(n1 * FFT_R + off, N2_CHUNK), :].astype(F32))

        def stage_a(j, carry):
            n2 = off + j
            xz = pz_ref[pl.ds(j, half, stride=P_PITCH), :].astype(BF16)
            xk = pk_ref[pl.ds(j, FFT_R, stride=P_PITCH), :].astype(BF16)
            rhs = jnp.concatenate([jnp.concatenate([xz, zeros_hi], axis=0), xk], axis=1)
            a = _dot(ta_ref[n2], rhs)
            row = pl.multiple_of(n2 * S_PITCH, 8)
            s_ref[0, pl.ds(row, FFT_R), :] = a[:, :tc]
            s_ref[1, pl.ds(row, FFT_R), :] = a[:, tc:]
            return carry

        lax.fori_loop(0, N2_CHUNK, stage_a, 0, unroll=FFT_UNROLL)

    def rows(k1):
        return pl.ds(k1, FFT_R, stride=S_PITCH), pl.ds(half + k1, FFT_R, stride=S_PITCH)

    def spectrum(k1):
        re_rows, im_rows = rows(k1)
        rhs = jnp.concatenate(
            [jnp.concatenate([s_ref[0, re_rows, :], s_ref[1, re_rows, :]], axis=1),
             jnp.concatenate([s_ref[0, im_rows, :], s_ref[1, im_rows, :]], axis=1)], axis=0).astype(BF16)
        return _dot(f2f_ref[...], rhs)

    def product(x):
        zr, zi = x[:FFT_R, :tc], x[FFT_R:, :tc]
        kr, ki = x[:FFT_R, tc:], x[FFT_R:, tc:]
        return jnp.concatenate([zr * kr - zi * ki, zr * ki + zi * kr], axis=0).astype(BF16)

    def stage_b(i, carry):
        k1s = [i * FFT_KB + u for u in range(FFT_KB)]
        xs = [spectrum(k1) for k1 in k1s]
        for u in range(0, FFT_KB, 2):
            d = _dot(f2i_ref[...], jnp.concatenate([product(xs[u]), product(xs[u + 1])], axis=1))
            for v in range(2):
                re_rows, im_rows = rows(k1s[u + v])
                s_ref[0, re_rows, :] = d[:FFT_R, v * tc:(v + 1) * tc]
                s_ref[0, im_rows, :] = d[FFT_R:, v * tc:(v + 1) * tc]
        return carry

    lax.fori_loop(0, half // FFT_KB, stage_b, 0)

    def stage_out(n2, carry):
        d = s_ref[0, pl.ds(pl.multiple_of(n2 * S_PITCH, 8), FFT_R), :].astype(BF16)
        y_ref[pl.ds(n2, half, stride=FFT_R), :] = _dot(ti_ref[n2], d)
        return carry

    lax.fori_loop(0, FFT_R, stage_out, 0, unroll=FFT_UNROLL)


def _fft_conv(z, kf, kb, tables):
    L, C = z.shape
    tc = FFT_TC
    ta, ti, f2f, f2i = tables
    col = pl.BlockSpec((L, tc), lambda j: (0, j))
    full = lambda a: pl.BlockSpec(a.shape, lambda j: (0,) * a.ndim)
    half = FFT_R // 2
    return pl.pallas_call(
        _fft_conv_kernel, grid=(C // tc,),
        in_specs=[col, col, col, full(ta), full(ti), full(f2f), full(f2i)],
        out_specs=col, out_shape=jax.ShapeDtypeStruct((L, C), F32),
        scratch_shapes=[pltpu.VMEM((half * P_PITCH, tc), F32), pltpu.VMEM((FFT_R * P_PITCH, tc), F32),
                        pltpu.VMEM((2, FFT_R * S_PITCH, tc), F32)],
        compiler_params=_params(1), name="fft_conv")(z, kf, kb, ta, ti, f2f, f2i)


def _outproj_kernel(a_ref, x0_ref, z_ref, yc_ref, norm_ref, skip_ref, ag_ref, hg_ref,
                    w_ref, b_ref, x_ref, g_ref, beta_ref, xf_ref, xb_ref, *, sub_rows):
    inv_norm = 1.0 / norm_ref[...]
    for r in range(a_ref.shape[0] // sub_rows):
        rows = pl.ds(r * sub_rows, sub_rows)
        a = _rms_norm(a_ref[rows, :].astype(F32), ag_ref[...])
        z = z_ref[rows, :].astype(F32)
        hy = x0_ref[rows, :].astype(F32) * (yc_ref[rows, :] * inv_norm + skip_ref[...] * z)
        h = _rms_norm(hy, hg_ref[...])
        mix = _dot(jnp.concatenate([a.astype(BF16), h.astype(BF16)], axis=1), w_ref[...]) + b_ref[...]
        y = _layer_norm(ALPHA * x_ref[rows, :] + mix, g_ref[...], beta_ref[...])
        xf_ref[rows, :] = y
        xb_ref[rows, :] = y.astype(BF16)


def _outproj(a, x0, z, yc, norm, skip, ag, hg, w_out, l, b_out, x, g, beta, tm=512, sub_rows=128):
    L, D = x.shape
    C = HYENA_WIDTH
    half = pl.BlockSpec((tm, C), lambda i: (i, 0))
    hvec = pl.BlockSpec((1, C), lambda i: (0, 0))
    row = pl.BlockSpec((tm, D), lambda i: (i, 0))
    vec = pl.BlockSpec((1, D), lambda i: (0, 0))
    w = pl.BlockSpec((None, ATT_WIDTH + C, D), lambda i: (l, 0, 0), pipeline_mode=pl.Buffered(1))
    return pl.pallas_call(
        functools.partial(_outproj_kernel, sub_rows=sub_rows), grid=(L // tm,),
        in_specs=[half, half, half, half, hvec, hvec, hvec, hvec, w, vec, row, vec, vec],
        out_specs=[row, row],
        out_shape=[jax.ShapeDtypeStruct((L, D), F32), jax.ShapeDtypeStruct((L, D), BF16)],
        compiler_params=_params(1), name="outproj_ln")(
            a, x0, z, yc, norm, skip, ag, hg, w_out, b_out, x, g, beta)


def _gelu_tanh(x):
    return x * (0.5 * (1.0 + jnp.tanh(math.sqrt(2.0 / math.pi) * (x + 0.044715 * (x * x * x)))))


def _ffn_up_kernel(x_ref, xp_ref, xn_ref, wg_ref, wv_ref, bg_ref, bv_ref, cw_ref, cb_ref, h_ref, xe_ref):
    _fill_extended(xe_ref, x_ref, xp_ref, xn_ref)
    gate = _conv3_extended(_dot(xe_ref[...], wg_ref[...]) + bg_ref[...], cw_ref, cb_ref)
    val = _dot(x_ref[...], wv_ref[...]) + bv_ref[...]
    h_ref[...] = (_gelu_tanh(gate) * val).astype(BF16)


def _ffn_up(xb, w_gate, w_val, b_gate, b_val, conv_w, conv_b, l, tm=1024, tn=512):
    L, K = xb.shape
    w = pl.BlockSpec((None, K, tn), lambda i, j: (l, 0, j))
    vec = pl.BlockSpec((1, tn), lambda i, j: (0, j))
    return pl.pallas_call(
        _ffn_up_kernel, grid=(L // tm, pl.cdiv(D_FF, tn)),
        in_specs=_halo_specs(tm, K, L) + [w, w, vec, vec, pl.BlockSpec((3, tn), lambda i, j: (0, j)), vec],
        out_specs=pl.BlockSpec((tm, tn), lambda i, j: (i, j)),
        out_shape=jax.ShapeDtypeStruct((L, D_FF), BF16),
        scratch_shapes=[pltpu.VMEM((tm + 2 * HALO, K), BF16)],
        compiler_params=_params(2), name="ffn_up")(xb, xb, xb, w_gate, w_val, b_gate, b_val, conv_w, conv_b)


def _ffn_down_kernel(h_ref, w_ref, b_ref, x_ref, g_ref, beta_ref, xf_ref, xb_ref):
    ffn = _dot(h_ref[...], w_ref[...]) + b_ref[...]
    y = _layer_norm(ALPHA * x_ref[...] + ffn, g_ref[...], beta_ref[...])
    xf_ref[...] = y
    xb_ref[...] = y.astype(BF16)


def _ffn_down(h, w_down, b_down, x, g, beta, l, tm=256):
    L, D = x.shape
    row = pl.BlockSpec((tm, D), lambda i: (i, 0))
    vec = pl.BlockSpec((1, D), lambda i: (0, 0))
    return pl.pallas_call(
        _ffn_down_kernel, grid=(L // tm,),
        in_specs=[pl.BlockSpec((tm, D_FF), lambda i: (i, 0)),
                  pl.BlockSpec((None, D_FF, D), lambda i: (l, 0, 0), pipeline_mode=pl.Buffered(1)),
                  vec, row, vec, vec],
        out_specs=[row, row],
        out_shape=[jax.ShapeDtypeStruct((L, D), F32), jax.ShapeDtypeStruct((L, D), BF16)],
        compiler_params=_params(1), name="ffn_down_ln")(h, w_down, b_down, x, g, beta)


def _rope_tables(L):
    rows = L // GRID_W
    row_pos = np.repeat(np.arange(rows, dtype=np.float64), GRID_W)
    col_pos = np.tile(np.arange(GRID_W, dtype=np.float64), rows)
    dims = HEAD_DIM // 2

    def axis_table(pos):
        inv = ROPE_THETA ** (-np.arange(0, dims, 2, dtype=np.float64) / dims)
        ang = pos[:, None] * inv[None, :]
        ang = np.concatenate([ang, ang], axis=-1)
        return np.cos(ang), np.sin(ang)

    cos_r, sin_r = axis_table(row_pos)
    cos_c, sin_c = axis_table(col_pos)
    cos = np.concatenate([cos_r, cos_c], axis=-1)
    sin = np.concatenate([sin_r, sin_c], axis=-1)
    first = (np.arange(HEAD_DIM) % dims) < dims // 2
    tabs = (cos, np.where(first, -sin, 0.0), np.where(first, 0.0, sin))
    return tuple(jnp.asarray(np.ascontiguousarray(t.T.astype(np.float32))) for t in tabs)


def kernel(x, ln_in_g, ln_in_b, w_in, q_norm_g, k_norm_g, hy_conv_w, hy_conv_b, filt_w1, filt_b1, filt_f1, filt_w2, filt_b2, filt_f2, filt_w3, filt_b3, filt_decay, hy_skip, att_out_g, hy_out_g, w_out, b_out, ln1_g, ln1_b, w_up, b_up, ffn_conv_w, ffn_conv_b, w_down, b_down, ln2_g, ln2_b):
    assert x.shape == (1, SEQ, D_MODEL) and w_in.shape == (DEPTH, D_MODEL, ATT_IN + HY_IN)
    rope_t = _rope_tables(SEQ)
    tables = _fft_tables()
    fband = jnp.linspace(1e-4, FILTER_BANDS - 1, FILTER_BANDS, dtype=F32)

    w_in_b = w_in.astype(BF16)
    w_out_b = w_out.astype(BF16)
    w_up_b = w_up.astype(BF16)
    w_val_b = w_up_b[:, :, D_FF:]
    w_down_b = w_down.astype(BF16)

    xf, xb = _input_ln(x[0], ln_in_g[None], ln_in_b[None])
    for l in range(DEPTH):
        qt, k, vt = _inproj_att(xb, w_in_b, l, q_norm_g[l], k_norm_g[l], rope_t)
        att = _attention(qt, k, vt)
        z, x0 = _inproj_hy(xb, w_in_b, l, hy_conv_w[l], hy_conv_b[l][None])
        gf, gr = _filt_feat(fband, filt_w1[l], filt_b1[l], filt_f1[l], filt_w2[l], filt_b2[l], filt_f2[l])
        kf, kb, norm = _filt_gen(gf, gr, filt_w3[l], filt_b3[l], filt_decay[l])
        yc = _fft_conv(z, kf, kb, tables)
        xf, xb = _outproj(att, x0, z, yc, norm, hy_skip[l][None], att_out_g[l][None], hy_out_g[l][None],
                          w_out_b, l, b_out[l][None], xf, ln1_g[l][None], ln1_b[l][None])
        h = _ffn_up(xb, w_up_b, w_val_b, b_up[l][None, :D_FF], b_up[l][None, D_FF:],
                    ffn_conv_w[l], ffn_conv_b[l][None], l)
        xf, xb = _ffn_down(h, w_down_b, b_down[l][None], xf, ln2_g[l][None], ln2_b[l][None], l)
    return xf[None]
```

```python
import functools
import math

import jax
import jax.numpy as jnp
import numpy as np
from jax import lax
from jax.experimental import pallas as pl
from jax.experimental.pallas import tpu as pltpu

F32 = jnp.float32
BF16 = jnp.bfloat16

D_MODEL = 2048
SEQ = 8192
DEPTH = 2
ATT_WIDTH = 1024
HYENA_WIDTH = 1024
HEAD_DIM = 128
N_KV_HEADS = 2
GQA_GROUP = 4
KV_WIDTH = N_KV_HEADS * HEAD_DIM
ATT_IN = ATT_WIDTH + 2 * KV_WIDTH
HY_IN = 3 * HYENA_WIDTH
FILTER_HIDDEN = 64
FILTER_BANDS = 16
D_FF = 5504
GRID_W = 64
ROPE_THETA = 10000.0
ALPHA = (2.0 * DEPTH) ** 0.25
LN_EPS = 1e-5
RMS_EPS = 1e-6

LANES = 128
BF16_SUBLANES = 16
VMEM_LIMIT = 56 * 1024 * 1024
ATT_CHUNK = 128

FFT_R = 128
FFT_N = FFT_R * FFT_R
FFT_TC = 128
S_PITCH = 136
N2_CHUNK = 32
P_PITCH = 40
FFT_UNROLL = 16
FFT_KB = 16

_dot = functools.partial(jnp.dot, preferred_element_type=F32)
_dot_hi = functools.partial(jnp.dot, preferred_element_type=F32, precision=lax.Precision.HIGHEST)


def _params(n_axes, flags=None):
    return pltpu.CompilerParams(dimension_semantics=("arbitrary",) * n_axes,
                                vmem_limit_bytes=VMEM_LIMIT, flags=flags)


def _layer_norm(v, g, b):
    mu = jnp.mean(v, axis=-1, keepdims=True)
    d = v - mu
    var = jnp.mean(d * d, axis=-1, keepdims=True)
    return d * lax.rsqrt(var + LN_EPS) * g + b


def _rms_norm(v, g):
    ms = jnp.mean(v * v, axis=-1, keepdims=True)
    return v * lax.rsqrt(ms + RMS_EPS) * g


def _ln_kernel(x_ref, g_ref, b_ref, xf_ref, xb_ref):
    y = _layer_norm(x_ref[...], g_ref[...], b_ref[...])
    xf_ref[...] = y
    xb_ref[...] = y.astype(BF16)


def _input_ln(x, g, b, tm=512):
    L, D = x.shape
    row = pl.BlockSpec((tm, D), lambda i: (i, 0))
    vec = pl.BlockSpec((1, D), lambda i: (0, 0))
    return pl.pallas_call(
        _ln_kernel, grid=(L // tm,), in_specs=[row, vec, vec], out_specs=[row, row],
        out_shape=[jax.ShapeDtypeStruct((L, D), F32), jax.ShapeDtypeStruct((L, D), BF16)],
        compiler_params=_params(1), name="input_ln")(x, g, b)


def _norm_rope_heads(acc, g_ref, cos_ref, sa_ref, sb_ref):
    tm = acc.shape[0]
    gain = jnp.tile(g_ref[...], (1, tm // LANES))
    for h in range(acc.shape[1] // HEAD_DIM):
        sl = slice(h * HEAD_DIM, (h + 1) * HEAD_DIM)
        xt = acc[:, sl].T
        hn = xt * lax.rsqrt(jnp.mean(xt * xt, axis=0, keepdims=True) + RMS_EPS) * gain
        rot = pltpu.roll(hn, HEAD_DIM - 32, 0) * sa_ref[...] + pltpu.roll(hn, 32, 0) * sb_ref[...]
        yield sl, hn * cos_ref[...] + rot


def _inproj_q_kernel(x_ref, w_ref, g_ref, cos_ref, sa_ref, sb_ref, qt_ref, *, scale):
    for sl, r in _norm_rope_heads(_dot(x_ref[...], w_ref[...]), g_ref, cos_ref, sa_ref, sb_ref):
        qt_ref[sl, :] = (r * scale).astype(BF16)


def _inproj_kv_kernel(x_ref, wk_ref, wv_ref, g_ref, cos_ref, sa_ref, sb_ref, k_ref, vt_ref):
    x = x_ref[...]
    for sl, r in _norm_rope_heads(_dot(x, wk_ref[...]), g_ref, cos_ref, sa_ref, sb_ref):
        k_ref[:, sl] = r.T.astype(BF16)
    vt_ref[...] = _dot(x, wv_ref[...]).T.astype(BF16)


def _inproj_att(xb, w_in_b, l, qg, kg, tabs_t, tm=1024):
    L, K = xb.shape
    tn = KV_WIDTH
    tq = 2 * tn
    n_q = ATT_WIDTH // tn
    x_spec = pl.BlockSpec((tm, K), lambda i, j: (i, 0))
    w_tile = lambda width, col: pl.BlockSpec((None, K, width), lambda i, j: (l, 0, col(j)))
    gain = pl.BlockSpec((HEAD_DIM, LANES), lambda i, j: (0, 0))
    gain_col = lambda g: jnp.broadcast_to(g.reshape(HEAD_DIM, 1), (HEAD_DIM, LANES))
    tab_t = pl.BlockSpec((HEAD_DIM, tm), lambda i, j: (0, i))
    q_scale = HEAD_DIM ** -0.5 * math.log2(math.e)
    qt = pl.pallas_call(
        functools.partial(_inproj_q_kernel, scale=q_scale), grid=(L // tm, ATT_WIDTH // tq),
        in_specs=[x_spec, w_tile(tq, lambda j: j), gain, tab_t, tab_t, tab_t],
        out_specs=pl.BlockSpec((tq, tm), lambda i, j: (j, i)),
        out_shape=jax.ShapeDtypeStruct((ATT_WIDTH, L), BF16),
        compiler_params=_params(2), name="inproj_q")(xb, w_in_b, gain_col(qg), *tabs_t)
    k, vt = pl.pallas_call(
        _inproj_kv_kernel, grid=(L // tm, 1),
        in_specs=[x_spec, w_tile(tn, lambda j: n_q), w_tile(tn, lambda j: n_q + 1), gain, tab_t, tab_t, tab_t],
        out_specs=[pl.BlockSpec((tm, tn), lambda i, j: (i, 0)), pl.BlockSpec((tn, tm), lambda i, j: (0, i))],
        out_shape=[jax.ShapeDtypeStruct((L, KV_WIDTH), BF16), jax.ShapeDtypeStruct((KV_WIDTH, L), BF16)],
        compiler_params=_params(2), name="inproj_kv")(xb, w_in_b, w_in_b, gain_col(kg), *tabs_t)
    return qt, k, vt


def _attn_kernel(qt_ref, qtn_ref, *refs, tiles, ahead):
    k_first, k_next = refs[:ahead], refs[ahead:ahead + tiles]
    vt_ref, o_ref, m_ref, l_ref, acc_ref = refs[ahead + tiles:ahead + tiles + 5]
    s_refs = refs[ahead + tiles + 5:ahead + 2 * tiles + 5]
    p_refs = refs[ahead + 2 * tiles + 5:]
    i = pl.program_id(1)
    j = pl.program_id(2)
    last = j == pl.num_programs(2) - 1
    tk, tq = s_refs[0].shape[1:]
    sub = 8
    chunks = [pl.ds(c * ATT_CHUNK, ATT_CHUNK) for c in range(tk // ATT_CHUNK)]
    heads = range(GQA_GROUP)

    def scores(k_ref, s_ref, qt):
        k = k_ref[...]
        for h in heads:
            s_ref[h] = _dot(k, qt[h * HEAD_DIM:(h + 1) * HEAD_DIM, :])

    def softmax_pv(s_ref, p_ref, vt):
        for h in heads:
            m_part = jnp.full((sub, tq), -jnp.inf, F32)
            for rows in chunks:
                m_part = jnp.maximum(m_part, jnp.max(s_ref[h, rows, :].reshape(ATT_CHUNK // sub, sub, tq), axis=0))
            m_prev = m_ref[h]
            m_new = jnp.maximum(m_prev, jnp.max(m_part, axis=0, keepdims=True))
            alpha = jnp.exp2(m_prev - m_new)
            l_part = jnp.zeros((sub, tq), F32)
            for rows in chunks:
                p = jnp.exp2(s_ref[h, rows, :] - m_new)
                l_part = l_part + jnp.sum(p.reshape(ATT_CHUNK // sub, sub, tq), axis=0)
                p_ref[h, rows, :] = p.astype(BF16)
            l_ref[h] = alpha * l_ref[h] + jnp.sum(l_part, axis=0, keepdims=True)
            acc_ref[h] = alpha * acc_ref[h] + _dot(vt, p_ref[h])
            m_ref[h] = m_new

    @pl.when(j == 0)
    def _():
        m_ref[...] = jnp.full(m_ref.shape, -jnp.inf, F32)
        l_ref[...] = jnp.zeros(l_ref.shape, F32)
        acc_ref[...] = jnp.zeros(acc_ref.shape, F32)

    @pl.when(jnp.logical_and(j == 0, i == 0))
    def _():
        for r in range(ahead):
            scores(k_first[r], s_refs[r], qt_ref[...])

    qt = qt_ref[...]
    qt_next = jnp.where(last, qtn_ref[...], qt)
    for r in range(tiles):
        scores(k_next[r], s_refs[(r + ahead) % tiles], qt if r + ahead < tiles else qt_next)
        softmax_pv(s_refs[r], p_refs[r], vt_ref[:, r * tk:(r + 1) * tk])

    @pl.when(last)
    def _():
        for h in heads:
            o_ref[:, h * HEAD_DIM:(h + 1) * HEAD_DIM] = (acc_ref[h] / l_ref[h]).T.astype(BF16)


def _attention(qt, k, vt, tq=256, tk=1024, tiles=2, ahead=1):
    L = k.shape[0]
    gw = GQA_GROUP * HEAD_DIM
    nk = L // tk
    nq = L // tq
    k_tile = lambda f: pl.BlockSpec((tk, HEAD_DIM), lambda g, i, j: (f(j), g))
    k_first = [k_tile(functools.partial(lambda j, r: r, r=r)) for r in range(ahead)]
    k_next = [k_tile(functools.partial(lambda j, r: lax.rem(tiles * j + r + ahead, nk), r=r)) for r in range(tiles)]
    return pl.pallas_call(
        functools.partial(_attn_kernel, tiles=tiles, ahead=ahead), grid=(N_KV_HEADS, nq, nk // tiles),
        in_specs=[pl.BlockSpec((gw, tq), lambda g, i, j: (g, i)),
                  pl.BlockSpec((gw, tq), lambda g, i, j: (g, jnp.minimum(i + 1, nq - 1)))]
        + k_first + k_next + [pl.BlockSpec((HEAD_DIM, tiles * tk), lambda g, i, j: (g, j))],
        out_specs=pl.BlockSpec((tq, gw), lambda g, i, j: (i, g)),
        out_shape=jax.ShapeDtypeStruct((L, ATT_WIDTH), BF16),
        scratch_shapes=[pltpu.VMEM((GQA_GROUP, 1, tq), F32)] * 2 + [pltpu.VMEM((GQA_GROUP, HEAD_DIM, tq), F32)]
        + [pltpu.VMEM((GQA_GROUP, tk, tq), F32)] * tiles + [pltpu.VMEM((GQA_GROUP, tk, tq), BF16)] * tiles,
        compiler_params=_params(3), name="attention")(qt, qt, *([k] * (ahead + tiles)), vt)


HALO = BF16_SUBLANES


def _halo_specs(tm, K, L):
    hb = tm // HALO
    n_h = L // HALO
    return [pl.BlockSpec((tm, K), lambda i, j: (i, 0)),
            pl.BlockSpec((HALO, K), lambda i, j: (jnp.maximum(i * hb - 1, 0), 0)),
            pl.BlockSpec((HALO, K), lambda i, j: (jnp.minimum((i + 1) * hb, n_h - 1), 0))]


def _fill_extended(xe_ref, x_ref, xp_ref, xn_ref):
    tm = x_ref.shape[0]

    @pl.when(pl.program_id(1) == 0)
    def _():
        xe_ref[0:HALO, :] = xp_ref[...]
        xe_ref[HALO:HALO + tm, :] = x_ref[...]
        xe_ref[HALO + tm:, :] = xn_ref[...]


def _conv3_extended(u, w_ref, b_ref):
    n_ext = u.shape[0]
    tm = n_ext - 2 * HALO
    i = pl.program_id(0)
    top = u[:HALO] * jnp.where(i > 0, 1.0, 0.0)
    bot = u[HALO + tm:] * jnp.where(i < pl.num_programs(0) - 1, 1.0, 0.0)
    u = jnp.concatenate([top, u[HALO:HALO + tm], bot], axis=0)
    um = pltpu.roll(u, 1, 0)[HALO:HALO + tm]
    up = pltpu.roll(u, n_ext - 1, 0)[HALO:HALO + tm]
    return w_ref[0:1, :] * um + w_ref[1:2, :] * u[HALO:HALO + tm] + w_ref[2:3, :] * up + b_ref[...]


def _inproj_hy_kernel(x_ref, xp_ref, xn_ref, w0_ref, w1_ref, w2_ref, cw0_ref, cw1_ref, cw2_ref,
                      cb0_ref, cb1_ref, cb2_ref, z_ref, x0_ref, xe_ref):
    _fill_extended(xe_ref, x_ref, xp_ref, xn_ref)
    xe = xe_ref[...]
    x0_ref[...] = _conv3_extended(_dot(xe, w0_ref[...]), cw0_ref, cb0_ref).astype(BF16)
    x1 = _conv3_extended(_dot(xe, w1_ref[...]), cw1_ref, cb1_ref)
    v = _conv3_extended(_dot(xe, w2_ref[...]), cw2_ref, cb2_ref)
    z_ref[...] = (x1 * v).astype(BF16)


def _inproj_hy(xb, w_in_b, l, conv_w, conv_b, tm=1024, tn=512):
    L, K = xb.shape
    C = HYENA_WIDTH
    nj = C // tn
    w_specs = [pl.BlockSpec((None, K, tn), functools.partial(lambda i, j, g: (l, 0, (ATT_IN + g * C) // tn + j), g=g))
               for g in range(3)]
    cw_specs = [pl.BlockSpec((3, tn), functools.partial(lambda i, j, g: (0, g * nj + j), g=g)) for g in range(3)]
    cb_specs = [pl.BlockSpec((1, tn), functools.partial(lambda i, j, g: (0, g * nj + j), g=g)) for g in range(3)]
    out = pl.BlockSpec((tm, tn), lambda i, j: (i, j))
    return pl.pallas_call(
        _inproj_hy_kernel, grid=(L // tm, nj),
        in_specs=_halo_specs(tm, K, L) + w_specs + cw_specs + cb_specs,
        out_specs=[out, out], out_shape=[jax.ShapeDtypeStruct((L, C), BF16)] * 2,
        scratch_shapes=[pltpu.VMEM((tm + 2 * HALO, K), BF16)],
        compiler_params=_params(2), name="inproj_hy")(
            xb, xb, xb, w_in_b, w_in_b, w_in_b, conv_w, conv_w, conv_w, conv_b, conv_b, conv_b)


def _filt_feat_kernel(fb_ref, w1t_ref, w1c_ref, w1s_ref, b1_ref, f1_ref, w2_ref, b2_ref, f2_ref,
                      gf_ref, gr_ref, *, tm):
    pos = lax.broadcasted_iota(jnp.int32, (1, tm), 1) + pl.program_id(0) * tm

    def feats(p):
        pf = p.astype(F32)
        ang = fb_ref[...] * ((2.0 * math.pi * pf) / SEQ)
        pre = (w1t_ref[...] * (pf / (SEQ - 1)) + _dot_hi(w1c_ref[...], jnp.cos(ang))
               + _dot_hi(w1s_ref[...], -jnp.sin(ang)) + b1_ref[...])
        h = jnp.sin(f1_ref[...] * pre)
        return jnp.sin(f2_ref[...] * (_dot_hi(w2_ref[...], h) + b2_ref[...]))

    gf_ref[...] = feats(pos)
    gr_ref[...] = feats(jnp.where(pos == 0, 0, SEQ - pos))


def _filt_feat(fband, w1, b1, f1, w2, b2, f2, tm=1024):
    H = FILTER_HIDDEN
    full = lambda a: pl.BlockSpec(a.shape, lambda i: (0,) * a.ndim)
    col = lambda v: v[:, None]
    args = [col(fband), col(w1[0]), w1[1:1 + FILTER_BANDS].T, w1[1 + FILTER_BANDS:].T, col(b1), col(f1),
            w2.T, col(b2), col(f2)]
    out = pl.BlockSpec((H, tm), lambda i: (0, i))
    return pl.pallas_call(
        functools.partial(_filt_feat_kernel, tm=tm), grid=(SEQ // tm,),
        in_specs=[full(a) for a in args], out_specs=[out, out],
        out_shape=[jax.ShapeDtypeStruct((H, SEQ), F32)] * 2,
        compiler_params=_params(1), name="filt_feat")(*args)


def _dot_t(a_t, b):
    return lax.dot_general(a_t.astype(BF16), b.astype(BF16), (((0,), (0,)), ((), ())), preferred_element_type=F32)


def _filt_gen_kernel(gf_ref, gr_ref, w3f_ref, w3b_ref, b3f_ref, b3b_ref, df_ref, db_ref,
                     kf_ref, kb_ref, norm_ref, *, tm):
    i = pl.program_id(0)
    pos = lax.broadcasted_iota(jnp.int32, (tm, HYENA_WIDTH), 0) + i * tm
    pos_r = jnp.where(pos == 0, 0, SEQ - pos)
    t = pos.astype(F32) / (SEQ - 1)
    tr = pos_r.astype(F32) / (SEQ - 1)
    hf = (_dot_t(gf_ref[...], w3f_ref[...]) + b3f_ref[...]) * jnp.exp(-t * jnp.abs(df_ref[...]))
    hb = (_dot_t(gr_ref[...], w3b_ref[...]) + b3b_ref[...]) * jnp.exp(-tr * jnp.abs(db_ref[...]))

    @pl.when(i == 0)
    def _():
        norm_ref[...] = jnp.zeros(norm_ref.shape, F32)

    norm_ref[...] += jnp.sum(jnp.abs(hf) + jnp.abs(hb), axis=0, keepdims=True)
    kf_ref[...] = hf.astype(BF16)
    kb_ref[...] = jnp.where(pos == 0, 0.0, -hb).astype(BF16)


def _filt_gen(gf, gr, w3, b3, decay, tm=512):
    C, H = HYENA_WIDTH, FILTER_HIDDEN
    g = pl.BlockSpec((H, tm), lambda i: (0, i))
    w = pl.BlockSpec((H, C), lambda i: (0, 0))
    vec = pl.BlockSpec((1, C), lambda i: (0, 0))
    out = pl.BlockSpec((tm, C), lambda i: (i, 0))
    return pl.pallas_call(
        functools.partial(_filt_gen_kernel, tm=tm), grid=(SEQ // tm,),
        in_specs=[g, g, w, w, vec, vec, vec, vec], out_specs=[out, out, vec],
        out_shape=[jax.ShapeDtypeStruct((SEQ, C), BF16)] * 2 + [jax.ShapeDtypeStruct((1, C), F32)],
        compiler_params=_params(1), name="filt_gen")(
            gf, gr, w3[:, :C], w3[:, C:], b3[None, :C], b3[None, C:], decay[0:1], decay[1:2])


def _fft_tables():
    N = FFT_N
    n2 = np.arange(FFT_R, dtype=np.int64)[:, None, None]
    k1 = np.arange(FFT_R // 2, dtype=np.int64)[None, :, None]
    n1 = np.arange(FFT_R, dtype=np.int64)[None, None, :]
    theta = (((FFT_R * n1 + n2) * (2 * k1 + 1)) % (2 * N)) * (math.pi / N)
    c, s = np.cos(theta), np.sin(theta)
    ta = np.concatenate([c, -s], axis=1)
    ti = np.transpose(ta[:, :, :FFT_R // 2], (0, 2, 1)) * (2.0 / N)
    j = np.arange(FFT_R, dtype=np.int64)
    phi = ((j[:, None] * j[None, :]) % FFT_R) * (2.0 * math.pi / FFT_R)
    cm, sm = np.cos(phi), np.sin(phi)
    f2f = np.block([[cm, sm], [-sm, cm]])
    f2i = np.block([[cm, -sm], [sm, cm]])
    return tuple(jnp.asarray(t.astype(np.float32)).astype(BF16) for t in (ta, ti, f2f, f2i))


def _fft_conv_kernel(z_ref, kf_ref, kb_ref, ta_ref, ti_ref, f2f_ref, f2i_ref, y_ref, pz_ref, pk_ref, s_ref):
    half = FFT_R // 2
    tc = z_ref.shape[1]
    zeros_hi = jnp.zeros((half, tc), BF16)

    for c in range(FFT_R // N2_CHUNK):
        off = c * N2_CHUNK
        for n1 in range(half):
            pz_ref[pl.ds(n1 * P_PITCH, N2_CHUNK), :] = z_ref[pl.ds(n1 * FFT_R + off, N2_CHUNK), :].astype(F32)
            pk_ref[pl.ds(n1 * P_PITCH, N2_CHUNK), :] = kf_ref[pl.ds(n1 * FFT_R + off, N2_CHUNK), :].astype(F32)
            pk_ref[pl.ds((half + n1) * P_PITCH, N2_CHUNK), :] = (
                kb_ref[pl.ds(n1 * FFT_R + off, N2_CHUNK), :].astype(F32))

        def stage_a(j, carry):
            n2 = off + j
            xz = pz_ref[pl.ds(j, half, stride=P_PITCH), :].astype(BF16)
            xk = pk_ref[pl.ds(j, FFT_R, stride=P_PITCH), :].astype(BF16)
            rhs = jnp.concatenate([jnp.concatenate([xz, zeros_hi], axis=0), xk], axis=1)
            a = _dot(ta_ref[n2], rhs)
            row = pl.multiple_of(n2 * S_PITCH, 8)
            s_ref[0, pl.ds(row, FFT_R), :] = a[:, :tc]
            s_ref[1, pl.ds(row, FFT_R), :] = a[:, tc:]
            return carry

        lax.fori_loop(0, N2_CHUNK, stage_a, 0, unroll=FFT_UNROLL)

    def rows(k1):
        return pl.ds(k1, FFT_R, stride=S_PITCH), pl.ds(half + k1, FFT_R, stride=S_PITCH)

    def spectrum(k1):
        re_rows, im_rows = rows(k1)
        rhs = jnp.concatenate(
            [jnp.concatenate([s_ref[0, re_rows, :], s_ref[1, re_rows, :]], axis=1),
             jnp.concatenate([s_ref[0, im_rows, :], s_ref[1, im_rows, :]], axis=1)], axis=0).astype(BF16)
        return _dot(f2f_ref[...], rhs)

    def product(x):
        zr, zi = x[:FFT_R, :tc], x[FFT_R:, :tc]
        kr, ki = x[:FFT_R, tc:], x[FFT_R:, tc:]
        return jnp.concatenate([zr * kr - zi * ki, zr * ki + zi * kr], axis=0).astype(BF16)

    def stage_b(i, carry):
        k1s = [i * FFT_KB + u for u in range(FFT_KB)]
        xs = [spectrum(k1) for k1 in k1s]
        for u in range(0, FFT_KB, 2):
            d = _dot(f2i_ref[...], jnp.concatenate([product(xs[u]), product(xs[u + 1])], axis=1))
            for v in range(2):
                re_rows, im_rows = rows(k1s[u + v])
                s_ref[0, re_rows, :] = d[:FFT_R, v * tc:(v + 1) * tc]
                s_ref[0, im_rows, :] = d[FFT_R:, v * tc:(v + 1) * tc]
        return carry

    lax.fori_loop(0, half // FFT_KB, stage_b, 0)

    def stage_out(n2, carry):
        d = s_ref[0, pl.ds(pl.multiple_of(n2 * S_PITCH, 8), FFT_R), :].astype(BF16)
        y_ref[pl.ds(n2, half, stride=FFT_R), :] = _dot(ti_ref[n2], d)
        return carry

    lax.fori_loop(0, FFT_R, stage_out, 0, unroll=FFT_UNROLL)


def _fft_conv(z, kf, kb, tables):
    L, C = z.shape
    tc = FFT_TC
    ta, ti, f2f, f2i = tables
    col = pl.BlockSpec((L, tc), lambda j: (0, j))
    full = lambda a: pl.BlockSpec(a.shape, lambda j: (0,) * a.ndim)
    half = FFT_R // 2
    return pl.pallas_call(
        _fft_conv_kernel, grid=(C // tc,),
        in_specs=[col, col, col, full(ta), full(ti), full(f2f), full(f2i)],
        out_specs=col, out_shape=jax.ShapeDtypeStruct((L, C), F32),
        scratch_shapes=[pltpu.VMEM((half * P_PITCH, tc), F32), pltpu.VMEM((FFT_R * P_PITCH, tc), F32),
                        pltpu.VMEM((2, FFT_R * S_PITCH, tc), F32)],
        compiler_params=_params(1), name="fft_conv")(z, kf, kb, ta, ti, f2f, f2i)


def _outproj_kernel(a_ref, x0_ref, z_ref, yc_ref, norm_ref, skip_ref, ag_ref, hg_ref,
                    w_ref, b_ref, x_ref, g_ref, beta_ref, xf_ref, xb_ref, *, sub_rows):
    inv_norm = 1.0 / norm_ref[...]
    for r in range(a_ref.shape[0] // sub_rows):
        rows = pl.ds(r * sub_rows, sub_rows)
        a = _rms_norm(a_ref[rows, :].astype(F32), ag_ref[...])
        z = z_ref[rows, :].astype(F32)
        hy = x0_ref[rows, :].astype(F32) * (yc_ref[rows, :] * inv_norm + skip_ref[...] * z)
        h = _rms_norm(hy, hg_ref[...])
        mix = _dot(jnp.concatenate([a.astype(BF16), h.astype(BF16)], axis=1), w_ref[...]) + b_ref[...]
        y = _layer_norm(ALPHA * x_ref[rows, :] + mix, g_ref[...], beta_ref[...])
        xf_ref[rows, :] = y
        xb_ref[rows, :] = y.astype(BF16)


def _outproj(a, x0, z, yc, norm, skip, ag, hg, w_out, l, b_out, x, g, beta, tm=512, sub_rows=128):
    L, D = x.shape
    C = HYENA_WIDTH
    half = pl.BlockSpec((tm, C), lambda i: (i, 0))
    hvec = pl.BlockSpec((1, C), lambda i: (0, 0))
    row = pl.BlockSpec((tm, D), lambda i: (i, 0))
    vec = pl.BlockSpec((1, D), lambda i: (0, 0))
    w = pl.BlockSpec((None, ATT_WIDTH + C, D), lambda i: (l, 0, 0), pipeline_mode=pl.Buffered(1))
    return pl.pallas_call(
        functools.partial(_outproj_kernel, sub_rows=sub_rows), grid=(L // tm,),
        in_specs=[half, half, half, half, hvec, hvec, hvec, hvec, w, vec, row, vec, vec],
        out_specs=[row, row],
        out_shape=[jax.ShapeDtypeStruct((L, D), F32), jax.ShapeDtypeStruct((L, D), BF16)],
        compiler_params=_params(1), name="outproj_ln")(
            a, x0, z, yc, norm, skip, ag, hg, w_out, b_out, x, g, beta)


def _gelu_tanh(x):
    return x * (0.5 * (1.0 + jnp.tanh(math.sqrt(2.0 / math.pi) * (x + 0.044715 * (x * x * x)))))


def _ffn_up_kernel(x_ref, xp_ref, xn_ref, wg_ref, wv_ref, bg_ref, bv_ref, cw_ref, cb_ref, h_ref, xe_ref):
    _fill_extended(xe_ref, x_ref, xp_ref, xn_ref)
    gate = _conv3_extended(_dot(xe_ref[...], wg_ref[...]) + bg_ref[...], cw_ref, cb_ref)
    val = _dot(x_ref[...], wv_ref[...]) + bv_ref[...]
    h_ref[...] = (_gelu_tanh(gate) * val).astype(BF16)


def _ffn_up(xb, w_gate, w_val, b_gate, b_val, conv_w, conv_b, l, tm=1024, tn=512):
    L, K = xb.shape
    w = pl.BlockSpec((None, K, tn), lambda i, j: (l, 0, j))
    vec = pl.BlockSpec((1, tn), lambda i, j: (0, j))
    return pl.pallas_call(
        _ffn_up_kernel, grid=(L // tm, pl.cdiv(D_FF, tn)),
        in_specs=_halo_specs(tm, K, L) + [w, w, vec, vec, pl.BlockSpec((3, tn), lambda i, j: (0, j)), vec],
        out_specs=pl.BlockSpec((tm, tn), lambda i, j: (i, j)),
        out_shape=jax.ShapeDtypeStruct((L, D_FF), BF16),
        scratch_shapes=[pltpu.VMEM((tm + 2 * HALO, K), BF16)],
        compiler_params=_params(2), name="ffn_up")(xb, xb, xb, w_gate, w_val, b_gate, b_val, conv_w, conv_b)


def _ffn_down_kernel(h_ref, w_ref, b_ref, x_ref, g_ref, beta_ref, xf_ref, xb_ref):
    ffn = _dot(h_ref[...], w_ref[...]) + b_ref[...]
    y = _layer_norm(ALPHA * x_ref[...] + ffn, g_ref[...], beta_ref[...])
    xf_ref[...] = y
    xb_ref[...] = y.astype(BF16)


def _ffn_down(h, w_down, b_down, x, g, beta, l, tm=256):
    L, D = x.shape
    row = pl.BlockSpec((tm, D), lambda i: (i, 0))
    vec = pl.BlockSpec((1, D), lambda i: (0, 0))
    return pl.pallas_call(
        _ffn_down_kernel, grid=(L // tm,),
        in_specs=[pl.BlockSpec((tm, D_FF), lambda i: (i, 0)),
                  pl.BlockSpec((None, D_FF, D), lambda i: (l, 0, 0), pipeline_mode=pl.Buffered(1)),
                  vec, row, vec, vec],
        out_specs=[row, row],
        out_shape=[jax.ShapeDtypeStruct((L, D), F32), jax.ShapeDtypeStruct((L, D), BF16)],
        compiler_params=_params(1), name="ffn_down_ln")(h, w_down, b_down, x, g, beta)


def _rope_tables(L):
    rows = L // GRID_W
    row_pos = np.repeat(np.arange(rows, dtype=np.float64), GRID_W)
    col_pos = np.tile(np.arange(GRID_W, dtype=np.float64), rows)
    dims = HEAD_DIM // 2

    def axis_table(pos):
        inv = ROPE_THETA ** (-np.arange(0, dims, 2, dtype=np.float64) / dims)
        ang = pos[:, None] * inv[None, :]
        ang = np.concatenate([ang, ang], axis=-1)
        return np.cos(ang), np.sin(ang)

    cos_r, sin_r = axis_table(row_pos)
    cos_c, sin_c = axis_table(col_pos)
    cos = np.concatenate([cos_r, cos_c], axis=-1)
    sin = np.concatenate([sin_r, sin_c], axis=-1)
    first = (np.arange(HEAD_DIM) % dims) < dims // 2
    tabs = (cos, np.where(first, -sin, 0.0), np.where(first, 0.0, sin))
    return tuple(jnp.asarray(np.ascontiguousarray(t.T.astype(np.float32))) for t in tabs)


def kernel(x, ln_in_g, ln_in_b, w_in, q_norm_g, k_norm_g, hy_conv_w, hy_conv_b, filt_w1, filt_b1, filt_f1, filt_w2, filt_b2, filt_f2, filt_w3, filt_b3, filt_decay, hy_skip, att_out_g, hy_out_g, w_out, b_out, ln1_g, ln1_b, w_up, b_up, ffn_conv_w, ffn_conv_b, w_down, b_down, ln2_g, ln2_b):
    assert x.shape == (1, SEQ, D_MODEL) and w_in.shape == (DEPTH, D_MODEL, ATT_IN + HY_IN)
    rope_t = _rope_tables(SEQ)
    tables = _fft_tables()
    fband = jnp.linspace(1e-4, FILTER_BANDS - 1, FILTER_BANDS, dtype=F32)

    w_in_b = w_in.astype(BF16)
    w_out_b = w_out.astype(BF16)
    w_up_b = w_up.astype(BF16)
    w_val_b = w_up_b[:, :, D_FF:]
    w_down_b = w_down.astype(BF16)

    xf, xb = _input_ln(x[0], ln_in_g[None], ln_in_b[None])
    for l in range(DEPTH):
        qt, k, vt = _inproj_att(xb, w_in_b, l, q_norm_g[l], k_norm_g[l], rope_t)
        att = _attention(qt, k, vt)
        z, x0 = _inproj_hy(xb, w_in_b, l, hy_conv_w[l], hy_conv_b[l][None])
        gf, gr = _filt_feat(fband, filt_w1[l], filt_b1[l], filt_f1[l], filt_w2[l], filt_b2[l], filt_f2[l])
        kf, kb, norm = _filt_gen(gf, gr, filt_w3[l], filt_b3[l], filt_decay[l])
        yc = _fft_conv(z, kf, kb, tables)
        xf, xb = _outproj(att, x0, z, yc, norm, hy_skip[l][None], att_out_g[l][None], hy_out_g[l][None],
                          w_out_b, l, b_out[l][None], xf, ln1_g[l][None], ln1_b[l][None])
        h = _ffn_up(xb, w_up_b, w_val_b, b_up[l][None, :D_FF], b_up[l][None, D_FF:],
                    ffn_conv_w[l], ffn_conv_b[l][None], l)
        xf, xb = _ffn_down(h, w_down_b, b_down[l][None], xf, ln2_g[l][None], ln2_b[l][None], l)
    return xf[None]
```

```python
import functools
import math

import jax
import jax.numpy as jnp
import numpy as np
from jax import lax
from jax.experimental import pallas as pl
from jax.experimental.pallas import tpu as pltpu

F32 = jnp.float32
BF16 = jnp.bfloat16

D_MODEL = 2048
SEQ = 8192
DEPTH = 2
ATT_WIDTH = 1024
HYENA_WIDTH = 1024
HEAD_DIM = 128
N_KV_HEADS = 2
GQA_GROUP = 4
KV_WIDTH = N_KV_HEADS * HEAD_DIM
ATT_IN = ATT_WIDTH + 2 * KV_WIDTH
HY_IN = 3 * HYENA_WIDTH
FILTER_HIDDEN = 64
FILTER_BANDS = 16
D_FF = 5504
GRID_W = 64
ROPE_THETA = 10000.0
ALPHA = (2.0 * DEPTH) ** 0.25
LN_EPS = 1e-5
RMS_EPS = 1e-6

LANES = 128
BF16_SUBLANES = 16
VMEM_LIMIT = 56 * 1024 * 1024
ATT_CHUNK = 128

FFT_R = 128
FFT_N = FFT_R * FFT_R
FFT_TC = 128
S_PITCH = 136
N2_CHUNK = 32
P_PITCH = 40
FFT_UNROLL = 16
FFT_KB = 16

_dot = functools.partial(jnp.dot, preferred_element_type=F32)
_dot_hi = functools.partial(jnp.dot, preferred_element_type=F32, precision=lax.Precision.HIGHEST)


def _params(n_axes):
    return pltpu.CompilerParams(dimension_semantics=("arbitrary",) * n_axes, vmem_limit_bytes=VMEM_LIMIT)


def _layer_norm(v, g, b):
    mu = jnp.mean(v, axis=-1, keepdims=True)
    d = v - mu
    var = jnp.mean(d * d, axis=-1, keepdims=True)
    return d * lax.rsqrt(var + LN_EPS) * g + b


def _rms_norm(v, g):
    ms = jnp.mean(v * v, axis=-1, keepdims=True)
    return v * lax.rsqrt(ms + RMS_EPS) * g


def _ln_kernel(x_ref, g_ref, b_ref, xf_ref, xb_ref):
    y = _layer_norm(x_ref[...], g_ref[...], b_ref[...])
    xf_ref[...] = y
    xb_ref[...] = y.astype(BF16)


def _input_ln(x, g, b, tm=512):
    L, D = x.shape
    row = pl.BlockSpec((tm, D), lambda i: (i, 0))
    vec = pl.BlockSpec((1, D), lambda i: (0, 0))
    return pl.pallas_call(
        _ln_kernel, grid=(L // tm,), in_specs=[row, vec, vec], out_specs=[row, row],
        out_shape=[jax.ShapeDtypeStruct((L, D), F32), jax.ShapeDtypeStruct((L, D), BF16)],
        compiler_params=_params(1), name="input_ln")(x, g, b)


def _norm_rope_heads(acc, g_ref, cos_ref, sa_ref, sb_ref):
    tm = acc.shape[0]
    gain = jnp.tile(g_ref[...], (1, tm // LANES))
    for h in range(acc.shape[1] // HEAD_DIM):
        sl = slice(h * HEAD_DIM, (h + 1) * HEAD_DIM)
        xt = acc[:, sl].T
        hn = xt * lax.rsqrt(jnp.mean(xt * xt, axis=0, keepdims=True) + RMS_EPS) * gain
        rot = pltpu.roll(hn, HEAD_DIM - 32, 0) * sa_ref[...] + pltpu.roll(hn, 32, 0) * sb_ref[...]
        yield sl, hn * cos_ref[...] + rot


def _inproj_q_kernel(x_ref, w_ref, g_ref, cos_ref, sa_ref, sb_ref, qt_ref, *, scale):
    for sl, r in _norm_rope_heads(_dot(x_ref[...], w_ref[...]), g_ref, cos_ref, sa_ref, sb_ref):
        qt_ref[sl, :] = (r * scale).astype(BF16)


def _inproj_kv_kernel(x_ref, wk_ref, wv_ref, g_ref, cos_ref, sa_ref, sb_ref, k_ref, vt_ref):
    x = x_ref[...]
    for sl, r in _norm_rope_heads(_dot(x, wk_ref[...]), g_ref, cos_ref, sa_ref, sb_ref):
        k_ref[:, sl] = r.T.astype(BF16)
    vt_ref[...] = _dot(x, wv_ref[...]).T.astype(BF16)


def _inproj_att(xb, w_in_b, l, qg, kg, tabs_t, tm=1024):
    L, K = xb.shape
    tn = KV_WIDTH
    tq = 2 * tn
    n_q = ATT_WIDTH // tn
    x_spec = pl.BlockSpec((tm, K), lambda i, j: (i, 0))
    w_tile = lambda width, col: pl.BlockSpec((None, K, width), lambda i, j: (l, 0, col(j)))
    gain = pl.BlockSpec((HEAD_DIM, LANES), lambda i, j: (0, 0))
    gain_col = lambda g: jnp.broadcast_to(g.reshape(HEAD_DIM, 1), (HEAD_DIM, LANES))
    tab_t = pl.BlockSpec((HEAD_DIM, tm), lambda i, j: (0, i))
    q_scale = HEAD_DIM ** -0.5 * math.log2(math.e)
    qt = pl.pallas_call(
        functools.partial(_inproj_q_kernel, scale=q_scale), grid=(L // tm, ATT_WIDTH // tq),
        in_specs=[x_spec, w_tile(tq, lambda j: j), gain, tab_t, tab_t, tab_t],
        out_specs=pl.BlockSpec((tq, tm), lambda i, j: (j, i)),
        out_shape=jax.ShapeDtypeStruct((ATT_WIDTH, L), BF16),
        compiler_params=_params(2), name="inproj_q")(xb, w_in_b, gain_col(qg), *tabs_t)
    k, vt = pl.pallas_call(
        _inproj_kv_kernel, grid=(L // tm, 1),
        in_specs=[x_spec, w_tile(tn, lambda j: n_q), w_tile(tn, lambda j: n_q + 1), gain, tab_t, tab_t, tab_t],
        out_specs=[pl.BlockSpec((tm, tn), lambda i, j: (i, 0)), pl.BlockSpec((tn, tm), lambda i, j: (0, i))],
        out_shape=[jax.ShapeDtypeStruct((L, KV_WIDTH), BF16), jax.ShapeDtypeStruct((KV_WIDTH, L), BF16)],
        compiler_params=_params(2), name="inproj_kv")(xb, w_in_b, w_in_b, gain_col(kg), *tabs_t)
    return qt, k, vt


def _attn_kernel(qt_ref, qtn_ref, *refs, tiles, ahead):
    k_first, k_next = refs[:ahead], refs[ahead:ahead + tiles]
    vt_ref, o_ref, m_ref, l_ref, acc_ref = refs[ahead + tiles:ahead + tiles + 5]
    s_refs = refs[ahead + tiles + 5:ahead + 2 * tiles + 5]
    p_refs = refs[ahead + 2 * tiles + 5:]
    i = pl.program_id(1)
    j = pl.program_id(2)
    last = j == pl.num_programs(2) - 1
    tk, tq = s_refs[0].shape[1:]
    sub = 8
    chunks = [pl.ds(c * ATT_CHUNK, ATT_CHUNK) for c in range(tk // ATT_CHUNK)]
    heads = range(GQA_GROUP)

    def scores(k_ref, s_ref, qt):
        k = k_ref[...]
        for h in heads:
            s_ref[h] = _dot(k, qt[h * HEAD_DIM:(h + 1) * HEAD_DIM, :])

    def softmax_pv(s_ref, p_ref, vt):
        for h in heads:
            m_part = jnp.full((sub, tq), -jnp.inf, F32)
            for rows in chunks:
                m_part = jnp.maximum(m_part, jnp.max(s_ref[h, rows, :].reshape(ATT_CHUNK // sub, sub, tq), axis=0))
            m_prev = m_ref[h]
            m_new = jnp.maximum(m_prev, jnp.max(m_part, axis=0, keepdims=True))
            alpha = jnp.exp2(m_prev - m_new)
            l_part = jnp.zeros((sub, tq), F32)
            for rows in chunks:
                p = jnp.exp2(s_ref[h, rows, :] - m_new)
                l_part = l_part + jnp.sum(p.reshape(ATT_CHUNK // sub, sub, tq), axis=0)
                p_ref[h, rows, :] = p.astype(BF16)
            l_ref[h] = alpha * l_ref[h] + jnp.sum(l_part, axis=0, keepdims=True)
            acc_ref[h] = alpha * acc_ref[h] + _dot(vt, p_ref[h])
            m_ref[h] = m_new

    @pl.when(j == 0)
    def _():
        m_ref[...] = jnp.full(m_ref.shape, -jnp.inf, F32)
        l_ref[...] = jnp.zeros(l_ref.shape, F32)
        acc_ref[...] = jnp.zeros(acc_ref.shape, F32)

    @pl.when(jnp.logical_and(j == 0, i == 0))
    def _():
        for r in range(ahead):
            scores(k_first[r], s_refs[r], qt_ref[...])

    qt = qt_ref[...]
    qt_next = jnp.where(last, qtn_ref[...], qt)
    for r in range(tiles):
        scores(k_next[r], s_refs[(r + ahead) % tiles], qt if r + ahead < tiles else qt_next)
        softmax_pv(s_refs[r], p_refs[r], vt_ref[:, r * tk:(r + 1) * tk])

    @pl.when(last)
    def _():
        for h in heads:
            o_ref[:, h * HEAD_DIM:(h + 1) * HEAD_DIM] = (acc_ref[h] / l_ref[h]).T.astype(BF16)


def _attention(qt, k, vt, tq=256, tk=1024, tiles=2, ahead=1):
    L = k.shape[0]
    gw = GQA_GROUP * HEAD_DIM
    nk = L // tk
    nq = L // tq
    k_tile = lambda f: pl.BlockSpec((tk, HEAD_DIM), lambda g, i, j: (f(j), g))
    k_first = [k_tile(functools.partial(lambda j, r: r, r=r)) for r in range(ahead)]
    k_next = [k_tile(functools.partial(lambda j, r: lax.rem(tiles * j + r + ahead, nk), r=r)) for r in range(tiles)]
    return pl.pallas_call(
        functools.partial(_attn_kernel, tiles=tiles, ahead=ahead), grid=(N_KV_HEADS, nq, nk // tiles),
        in_specs=[pl.BlockSpec((gw, tq), lambda g, i, j: (g, i)),
                  pl.BlockSpec((gw, tq), lambda g, i, j: (g, jnp.minimum(i + 1, nq - 1)))]
        + k_first + k_next + [pl.BlockSpec((HEAD_DIM, tiles * tk), lambda g, i, j: (g, j))],
        out_specs=pl.BlockSpec((tq, gw), lambda g, i, j: (i, g)),
        out_shape=jax.ShapeDtypeStruct((L, ATT_WIDTH), BF16),
        scratch_shapes=[pltpu.VMEM((GQA_GROUP, 1, tq), F32)] * 2 + [pltpu.VMEM((GQA_GROUP, HEAD_DIM, tq), F32)]
        + [pltpu.VMEM((GQA_GROUP, tk, tq), F32)] * tiles + [pltpu.VMEM((GQA_GROUP, tk, tq), BF16)] * tiles,
        compiler_params=_params(3), name="attention")(qt, qt, *([k] * (ahead + tiles)), vt)


HALO = BF16_SUBLANES


def _halo_specs(tm, K, L):
    hb = tm // HALO
    n_h = L // HALO
    return [pl.BlockSpec((tm, K), lambda i, j: (i, 0)),
            pl.BlockSpec((HALO, K), lambda i, j: (jnp.maximum(i * hb - 1, 0), 0)),
            pl.BlockSpec((HALO, K), lambda i, j: (jnp.minimum((i + 1) * hb, n_h - 1), 0))]


def _fill_extended(xe_ref, x_ref, xp_ref, xn_ref):
    tm = x_ref.shape[0]

    @pl.when(pl.program_id(1) == 0)
    def _():
        xe_ref[0:HALO, :] = xp_ref[...]
        xe_ref[HALO:HALO + tm, :] = x_ref[...]
        xe_ref[HALO + tm:, :] = xn_ref[...]


def _conv3_extended(u, w_ref, b_ref):
    n_ext = u.shape[0]
    tm = n_ext - 2 * HALO
    i = pl.program_id(0)
    top = u[:HALO] * jnp.where(i > 0, 1.0, 0.0)
    bot = u[HALO + tm:] * jnp.where(i < pl.num_programs(0) - 1, 1.0, 0.0)
    u = jnp.concatenate([top, u[HALO:HALO + tm], bot], axis=0)
    um = pltpu.roll(u, 1, 0)[HALO:HALO + tm]
    up = pltpu.roll(u, n_ext - 1, 0)[HALO:HALO + tm]
    return w_ref[0:1, :] * um + w_ref[1:2, :] * u[HALO:HALO + tm] + w_ref[2:3, :] * up + b_ref[...]


def _inproj_hy_kernel(x_ref, xp_ref, xn_ref, w0_ref, w1_ref, w2_ref, cw0_ref, cw1_ref, cw2_ref,
                      cb0_ref, cb1_ref, cb2_ref, z_ref, x0_ref, xe_ref):
    _fill_extended(xe_ref, x_ref, xp_ref, xn_ref)
    xe = xe_ref[...]
    x0_ref[...] = _conv3_extended(_dot(xe, w0_ref[...]), cw0_ref, cb0_ref).astype(BF16)
    x1 = _conv3_extended(_dot(xe, w1_ref[...]), cw1_ref, cb1_ref)
    v = _conv3_extended(_dot(xe, w2_ref[...]), cw2_ref, cb2_ref)
    z_ref[...] = (x1 * v).astype(BF16)


def _inproj_hy(xb, w_in_b, l, conv_w, conv_b, tm=1024, tn=512):
    L, K = xb.shape
    C = HYENA_WIDTH
    nj = C // tn
    w_specs = [pl.BlockSpec((None, K, tn), functools.partial(lambda i, j, g: (l, 0, (ATT_IN + g * C) // tn + j), g=g))
               for g in range(3)]
    cw_specs = [pl.BlockSpec((3, tn), functools.partial(lambda i, j, g: (0, g * nj + j), g=g)) for g in range(3)]
    cb_specs = [pl.BlockSpec((1, tn), functools.partial(lambda i, j, g: (0, g * nj + j), g=g)) for g in range(3)]
    out = pl.BlockSpec((tm, tn), lambda i, j: (i, j))
    return pl.pallas_call(
        _inproj_hy_kernel, grid=(L // tm, nj),
        in_specs=_halo_specs(tm, K, L) + w_specs + cw_specs + cb_specs,
        out_specs=[out, out], out_shape=[jax.ShapeDtypeStruct((L, C), BF16)] * 2,
        scratch_shapes=[pltpu.VMEM((tm + 2 * HALO, K), BF16)],
        compiler_params=_params(2), name="inproj_hy")(
            xb, xb, xb, w_in_b, w_in_b, w_in_b, conv_w, conv_w, conv_w, conv_b, conv_b, conv_b)


def _filt_feat_kernel(fb_ref, w1t_ref, w1c_ref, w1s_ref, b1_ref, f1_ref, w2_ref, b2_ref, f2_ref,
                      gf_ref, gr_ref, *, tm):
    pos = lax.broadcasted_iota(jnp.int32, (1, tm), 1) + pl.program_id(0) * tm

    def feats(p):
        pf = p.astype(F32)
        ang = fb_ref[...] * ((2.0 * math.pi * pf) / SEQ)
        pre = (w1t_ref[...] * (pf / (SEQ - 1)) + _dot_hi(w1c_ref[...], jnp.cos(ang))
               + _dot_hi(w1s_ref[...], -jnp.sin(ang)) + b1_ref[...])
        h = jnp.sin(f1_ref[...] * pre)
        return jnp.sin(f2_ref[...] * (_dot_hi(w2_ref[...], h) + b2_ref[...]))

    gf_ref[...] = feats(pos)
    gr_ref[...] = feats(jnp.where(pos == 0, 0, SEQ - pos))


def _filt_feat(fband, w1, b1, f1, w2, b2, f2, tm=1024):
    H = FILTER_HIDDEN
    full = lambda a: pl.BlockSpec(a.shape, lambda i: (0,) * a.ndim)
    col = lambda v: v[:, None]
    args = [col(fband), col(w1[0]), w1[1:1 + FILTER_BANDS].T, w1[1 + FILTER_BANDS:].T, col(b1), col(f1),
            w2.T, col(b2), col(f2)]
    out = pl.BlockSpec((H, tm), lambda i: (0, i))
    return pl.pallas_call(
        functools.partial(_filt_feat_kernel, tm=tm), grid=(SEQ // tm,),
        in_specs=[full(a) for a in args], out_specs=[out, out],
        out_shape=[jax.ShapeDtypeStruct((H, SEQ), F32)] * 2,
        compiler_params=_params(1), name="filt_feat")(*args)


def _dot_t(a_t, b):
    return lax.dot_general(a_t.astype(BF16), b.astype(BF16), (((0,), (0,)), ((), ())), preferred_element_type=F32)


def _filt_gen_kernel(gf_ref, gr_ref, w3f_ref, w3b_ref, b3f_ref, b3b_ref, df_ref, db_ref,
                     kf_ref, kb_ref, norm_ref, *, tm):
    i = pl.program_id(0)
    pos = lax.broadcasted_iota(jnp.int32, (tm, HYENA_WIDTH), 0) + i * tm
    pos_r = jnp.where(pos == 0, 0, SEQ - pos)
    t = pos.astype(F32) / (SEQ - 1)
    tr = pos_r.astype(F32) / (SEQ - 1)
    hf = (_dot_t(gf_ref[...], w3f_ref[...]) + b3f_ref[...]) * jnp.exp(-t * jnp.abs(df_ref[...]))
    hb = (_dot_t(gr_ref[...], w3b_ref[...]) + b3b_ref[...]) * jnp.exp(-tr * jnp.abs(db_ref[...]))

    @pl.when(i == 0)
    def _():
        norm_ref[...] = jnp.zeros(norm_ref.shape, F32)

    norm_ref[...] += jnp.sum(jnp.abs(hf) + jnp.abs(hb), axis=0, keepdims=True)
    kf_ref[...] = hf.astype(BF16)
    kb_ref[...] = jnp.where(pos == 0, 0.0, -hb).astype(BF16)


def _filt_gen(gf, gr, w3, b3, decay, tm=512):
    C, H = HYENA_WIDTH, FILTER_HIDDEN
    g = pl.BlockSpec((H, tm), lambda i: (0, i))
    w = pl.BlockSpec((H, C), lambda i: (0, 0))
    vec = pl.BlockSpec((1, C), lambda i: (0, 0))
    out = pl.BlockSpec((tm, C), lambda i: (i, 0))
    return pl.pallas_call(
        functools.partial(_filt_gen_kernel, tm=tm), grid=(SEQ // tm,),
        in_specs=[g, g, w, w, vec, vec, vec, vec], out_specs=[out, out, vec],
        out_shape=[jax.ShapeDtypeStruct((SEQ, C), BF16)] * 2 + [jax.ShapeDtypeStruct((1, C), F32)],
        compiler_params=_params(1), name="filt_gen")(
            gf, gr, w3[:, :C], w3[:, C:], b3[None, :C], b3[None, C:], decay[0:1], decay[1:2])


def _fft_tables():
    N = FFT_N
    n2 = np.arange(FFT_R, dtype=np.int64)[:, None, None]
    k1 = np.arange(FFT_R // 2, dtype=np.int64)[None, :, None]
    n1 = np.arange(FFT_R, dtype=np.int64)[None, None, :]
    theta = (((FFT_R * n1 + n2) * (2 * k1 + 1)) % (2 * N)) * (math.pi / N)
    c, s = np.cos(theta), np.sin(theta)
    ta = np.concatenate([c, -s], axis=1)
    ti = np.transpose(ta[:, :, :FFT_R // 2], (0, 2, 1)) * (2.0 / N)
    j = np.arange(FFT_R, dtype=np.int64)
    phi = ((j[:, None] * j[None, :]) % FFT_R) * (2.0 * math.pi / FFT_R)
    cm, sm = np.cos(phi), np.sin(phi)
    f2f = np.block([[cm, sm], [-sm, cm]])
    f2i = np.block([[cm, -sm], [sm, cm]])
    return tuple(jnp.asarray(t.astype(np.float32)).astype(BF16) for t in (ta, ti, f2f, f2i))


def _fft_conv_kernel(z_ref, kf_ref, kb_ref, ta_ref, ti_ref, f2f_ref, f2i_ref, y_ref, pz_ref, pk_ref, s_ref):
    half = FFT_R // 2
    tc = z_ref.shape[1]
    zeros_hi = jnp.zeros((half, tc), BF16)

    for c in range(FFT_R // N2_CHUNK):
        off = c * N2_CHUNK
        for n1 in range(half):
            pz_ref[pl.ds(n1 * P_PITCH, N2_CHUNK), :] = z_ref[pl.ds(n1 * FFT_R + off, N2_CHUNK), :].astype(F32)
            pk_ref[pl.ds(n1 * P_PITCH, N2_CHUNK), :] = kf_ref[pl.ds(n1 * FFT_R + off, N2_CHUNK), :].astype(F32)
            pk_ref[pl.ds((half + n1) * P_PITCH, N2_CHUNK), :] = (
                kb_ref[pl.ds(n1 * FFT_R + off, N2_CHUNK), :].astype(F32))

        def stage_a(j, carry):
            n2 = off + j
            xz = pz_ref[pl.ds(j, half, stride=P_PITCH), :].astype(BF16)
            xk = pk_ref[pl.ds(j, FFT_R, stride=P_PITCH), :].astype(BF16)
            rhs = jnp.concatenate([jnp.concatenate([xz, zeros_hi], axis=0), xk], axis=1)
            a = _dot(ta_ref[n2], rhs)
            row = pl.multiple_of(n2 * S_PITCH, 8)
            s_ref[0, pl.ds(row, FFT_R), :] = a[:, :tc]
            s_ref[1, pl.ds(row, FFT_R), :] = a[:, tc:]
            return carry

        lax.fori_loop(0, N2_CHUNK, stage_a, 0, unroll=FFT_UNROLL)

    def rows(k1):
        return pl.ds(k1, FFT_R, stride=S_PITCH), pl.ds(half + k1, FFT_R, stride=S_PITCH)

    def spectrum(k1):
        re_rows, im_rows = rows(k1)
        rhs = jnp.concatenate(
            [jnp.concatenate([s_ref[0, re_rows, :], s_ref[1, re_rows, :]], axis=1),
             jnp.concatenate([s_ref[0, im_rows, :], s_ref[1, im_rows, :]], axis=1)], axis=0).astype(BF16)
        return _dot(f2f_ref[...], rhs)

    def product(x):
        zr, zi = x[:FFT_R, :tc], x[FFT_R:, :tc]
        kr, ki = x[:FFT_R, tc:], x[FFT_R:, tc:]
        return jnp.concatenate([zr * kr - zi * ki, zr * ki + zi * kr], axis=0).astype(BF16)

    def stage_b(i, carry):
        k1s = [i * FFT_KB + u for u in range(FFT_KB)]
        xs = [spectrum(k1) for k1 in k1s]
        for u in range(0, FFT_KB, 2):
            d = _dot(f2i_ref[...], jnp.concatenate([product(xs[u]), product(xs[u + 1])], axis=1))
            for v in range(2):
                re_rows, im_rows = rows(k1s[u + v])
                s_ref[0, re_rows, :] = d[:FFT_R, v * tc:(v + 1) * tc]
                s_ref[0, im_rows, :] = d[FFT_R:, v * tc:(v + 1) * tc]
        return carry

    lax.fori_loop(0, half // FFT_KB, stage_b, 0)

    def stage_out(n2, carry):
        d = s_ref[0, pl.ds(pl.multiple_of(n2 * S_PITCH, 8), FFT_R), :].astype(BF16)
        y_ref[pl.ds(n2, half, stride=FFT_R), :] = _dot(ti_ref[n2], d)
        return carry

    lax.fori_loop(0, FFT_R, stage_out, 0, unroll=FFT_UNROLL)


def _fft_conv(z, kf, kb, tables):
    L, C = z.shape
    tc = FFT_TC
    ta, ti, f2f, f2i = tables
    col = pl.BlockSpec((L, tc), lambda j: (0, j))
    full = lambda a: pl.BlockSpec(a.shape, lambda j: (0,) * a.ndim)
    half = FFT_R // 2
    return pl.pallas_call(
        _fft_conv_kernel, grid=(C // tc,),
        in_specs=[col, col, col, full(ta), full(ti), full(f2f), full(f2i)],
        out_specs=col, out_shape=jax.ShapeDtypeStruct((L, C), F32),
        scratch_shapes=[pltpu.VMEM((half * P_PITCH, tc), F32), pltpu.VMEM((FFT_R * P_PITCH, tc), F32),
                        pltpu.VMEM((2, FFT_R * S_PITCH, tc), F32)],
        compiler_params=_params(1), name="fft_conv")(z, kf, kb, ta, ti, f2f, f2i)


def _outproj_kernel(a_ref, x0_ref, z_ref, yc_ref, norm_ref, skip_ref, ag_ref, hg_ref,
                    w_ref, b_ref, x_ref, g_ref, beta_ref, xf_ref, xb_ref, *, sub_rows):
    inv_norm = 1.0 / norm_ref[...]
    for r in range(a_ref.shape[0] // sub_rows):
        rows = pl.ds(r * sub_rows, sub_rows)
        a = _rms_norm(a_ref[rows, :].astype(F32), ag_ref[...])
        z = z_ref[rows, :].astype(F32)
        hy = x0_ref[rows, :].astype(F32) * (yc_ref[rows, :] * inv_norm + skip_ref[...] * z)
        h = _rms_norm(hy, hg_ref[...])
        mix = _dot(jnp.concatenate([a.astype(BF16), h.astype(BF16)], axis=1), w_ref[...]) + b_ref[...]
        y = _layer_norm(ALPHA * x_ref[rows, :] + mix, g_ref[...], beta_ref[...])
        xf_ref[rows, :] = y
        xb_ref[rows, :] = y.astype(BF16)


def _outproj(a, x0, z, yc, norm, skip, ag, hg, w_out, l, b_out, x, g, beta, tm=512, sub_rows=128):
    L, D = x.shape
    C = HYENA_WIDTH
    half = pl.BlockSpec((tm, C), lambda i: (i, 0))
    hvec = pl.BlockSpec((1, C), lambda i: (0, 0))
    row = pl.BlockSpec((tm, D), lambda i: (i, 0))
    vec = pl.BlockSpec((1, D), lambda i: (0, 0))
    w = pl.BlockSpec((None, ATT_WIDTH + C, D), lambda i: (l, 0, 0), pipeline_mode=pl.Buffered(1))
    return pl.pallas_call(
        functools.partial(_outproj_kernel, sub_rows=sub_rows), grid=(L // tm,),
        in_specs=[half, half, half, half, hvec, hvec, hvec, hvec, w, vec, row, vec, vec],
        out_specs=[row, row],
        out_shape=[jax.ShapeDtypeStruct((L, D), F32), jax.ShapeDtypeStruct((L, D), BF16)],
        compiler_params=_params(1), name="outproj_ln")(
            a, x0, z, yc, norm, skip, ag, hg, w_out, b_out, x, g, beta)


def _gelu_tanh(x):
    return x * (0.5 * (1.0 + jnp.tanh(math.sqrt(2.0 / math.pi) * (x + 0.044715 * (x * x * x)))))


def _ffn_up_kernel(x_ref, xp_ref, xn_ref, wg_ref, *refs):
    wv_refs, (bg_ref, bv_ref, cw_ref, cb_ref, h_ref, xe_ref) = refs[:-6], refs[-6:]
    _fill_extended(xe_ref, x_ref, xp_ref, xn_ref)
    gate = _conv3_extended(_dot(xe_ref[...], wg_ref[...]) + bg_ref[...], cw_ref, cb_ref)
    val = _dot(x_ref[...], jnp.concatenate([r[...] for r in wv_refs], axis=1)) + bv_ref[...]
    h_ref[...] = (_gelu_tanh(gate) * val).astype(BF16)


def _ffn_up(xb, w_up_b, b_gate, b_val, conv_w, conv_b, l, tm=1024, tn=512):
    L, K = xb.shape
    n_lane_blocks = w_up_b.shape[2] // LANES
    val0 = D_FF // LANES
    w_val = [pl.BlockSpec((None, K, LANES), functools.partial(
        lambda i, j, r: (l, 0, jnp.minimum(val0 + (tn // LANES) * j + r, n_lane_blocks - 1)), r=r))
        for r in range(tn // LANES)]
    vec = pl.BlockSpec((1, tn), lambda i, j: (0, j))
    return pl.pallas_call(
        _ffn_up_kernel, grid=(L // tm, pl.cdiv(D_FF, tn)),
        in_specs=_halo_specs(tm, K, L) + [pl.BlockSpec((None, K, tn), lambda i, j: (l, 0, j))] + w_val
        + [vec, vec, pl.BlockSpec((3, tn), lambda i, j: (0, j)), vec],
        out_specs=pl.BlockSpec((tm, tn), lambda i, j: (i, j)),
        out_shape=jax.ShapeDtypeStruct((L, D_FF), BF16),
        scratch_shapes=[pltpu.VMEM((tm + 2 * HALO, K), BF16)],
        compiler_params=_params(2), name="ffn_up")(
            xb, xb, xb, w_up_b, *([w_up_b] * (tn // LANES)), b_gate, b_val, conv_w, conv_b)


def _ffn_down_kernel(h_ref, w_ref, b_ref, x_ref, g_ref, beta_ref, xf_ref, xb_ref):
    ffn = _dot(h_ref[...], w_ref[...]) + b_ref[...]
    y = _layer_norm(ALPHA * x_ref[...] + ffn, g_ref[...], beta_ref[...])
    xf_ref[...] = y
    xb_ref[...] = y.astype(BF16)


def _ffn_down(h, w_down, b_down, x, g, beta, l, tm=256):
    L, D = x.shape
    row = pl.BlockSpec((tm, D), lambda i: (i, 0))
    vec = pl.BlockSpec((1, D), lambda i: (0, 0))
    return pl.pallas_call(
        _ffn_down_kernel, grid=(L // tm,),
        in_specs=[pl.BlockSpec((tm, D_FF), lambda i: (i, 0)),
                  pl.BlockSpec((None, D_FF, D), lambda i: (l, 0, 0), pipeline_mode=pl.Buffered(1)),
                  vec, row, vec, vec],
        out_specs=[row, row],
        out_shape=[jax.ShapeDtypeStruct((L, D), F32), jax.ShapeDtypeStruct((L, D), BF16)],
        compiler_params=_params(1), name="ffn_down_ln")(h, w_down, b_down, x, g, beta)


def _rope_tables(L):
    rows = L // GRID_W
    row_pos = np.repeat(np.arange(rows, dtype=np.float64), GRID_W)
    col_pos = np.tile(np.arange(GRID_W, dtype=np.float64), rows)
    dims = HEAD_DIM // 2

    def axis_table(pos):
        inv = ROPE_THETA ** (-np.arange(0, dims, 2, dtype=np.float64) / dims)
        ang = pos[:, None] * inv[None, :]
        ang = np.concatenate([ang, ang], axis=-1)
        return np.cos(ang), np.sin(ang)

    cos_r, sin_r = axis_table(row_pos)
    cos_c, sin_c = axis_table(col_pos)
    cos = np.concatenate([cos_r, cos_c], axis=-1)
    sin = np.concatenate([sin_r, sin_c], axis=-1)
    first = (np.arange(HEAD_DIM) % dims) < dims // 2
    tabs = (cos, np.where(first, -sin, 0.0), np.where(first, 0.0, sin))
    return tuple(jnp.asarray(np.ascontiguousarray(t.T.astype(np.float32))) for t in tabs)


def kernel(x, ln_in_g, ln_in_b, w_in, q_norm_g, k_norm_g, hy_conv_w, hy_conv_b, filt_w1, filt_b1, filt_f1, filt_w2, filt_b2, filt_f2, filt_w3, filt_b3, filt_decay, hy_skip, att_out_g, hy_out_g, w_out, b_out, ln1_g, ln1_b, w_up, b_up, ffn_conv_w, ffn_conv_b, w_down, b_down, ln2_g, ln2_b):
    assert x.shape == (1, SEQ, D_MODEL) and w_in.shape == (DEPTH, D_MODEL, ATT_IN + HY_IN)
    rope_t = _rope_tables(SEQ)
    tables = _fft_tables()
    fband = jnp.linspace(1e-4, FILTER_BANDS - 1, FILTER_BANDS, dtype=F32)

    w_in_b = w_in.astype(BF16)
    w_out_b = w_out.astype(BF16)
    w_up_b = w_up.astype(BF16)
    w_down_b = w_down.astype(BF16)

    xf, xb = _input_ln(x[0], ln_in_g[None], ln_in_b[None])
    for l in range(DEPTH):
        qt, k, vt = _inproj_att(xb, w_in_b, l, q_norm_g[l], k_norm_g[l], rope_t)
        att = _attention(qt, k, vt)
        z, x0 = _inproj_hy(xb, w_in_b, l, hy_conv_w[l], hy_conv_b[l][None])
        gf, gr = _filt_feat(fband, filt_w1[l], filt_b1[l], filt_f1[l], filt_w2[l], filt_b2[l], filt_f2[l])
        kf, kb, norm = _filt_gen(gf, gr, filt_w3[l], filt_b3[l], filt_decay[l])
        yc = _fft_conv(z, kf, kb, tables)
        xf, xb = _outproj(att, x0, z, yc, norm, hy_skip[l][None], att_out_g[l][None], hy_out_g[l][None],
                          w_out_b, l, b_out[l][None], xf, ln1_g[l][None], ln1_b[l][None])
        h = _ffn_up(xb, w_up_b, b_up[l][None, :D_FF], b_up[l][None, D_FF:],
                    ffn_conv_w[l], ffn_conv_b[l][None], l)
        xf, xb = _ffn_down(h, w_down_b, b_down[l][None], xf, ln2_g[l][None], ln2_b[l][None], l)
    return xf[None]
```

```python
import functools
import math

import jax
import jax.numpy as jnp
import numpy as np
from jax import lax
from jax.experimental import pallas as pl
from jax.experimental.pallas import tpu as pltpu

F32 = jnp.float32
BF16 = jnp.bfloat16

D_MODEL = 2048
SEQ = 8192
DEPTH = 2
ATT_WIDTH = 1024
HYENA_WIDTH = 1024
HEAD_DIM = 128
N_KV_HEADS = 2
GQA_GROUP = 4
KV_WIDTH = N_KV_HEADS * HEAD_DIM
ATT_IN = ATT_WIDTH + 2 * KV_WIDTH
HY_IN = 3 * HYENA_WIDTH
FILTER_HIDDEN = 64
FILTER_BANDS = 16
D_FF = 5504
GRID_W = 64
ROPE_THETA = 10000.0
ALPHA = (2.0 * DEPTH) ** 0.25
LN_EPS = 1e-5
RMS_EPS = 1e-6

LANES = 128
BF16_SUBLANES = 16
VMEM_LIMIT = 56 * 1024 * 1024
ATT_CHUNK = 128

FFT_R = 128
FFT_N = FFT_R * FFT_R
FFT_TC = 128
S_PITCH = 136
N2_CHUNK = 32
P_PITCH = 40
FFT_UNROLL = 16
FFT_KB = 16

_dot = functools.partial(jnp.dot, preferred_element_type=F32)
_dot_hi = functools.partial(jnp.dot, preferred_element_type=F32, precision=lax.Precision.HIGHEST)


def _params(n_axes):
    return pltpu.CompilerParams(dimension_semantics=("arbitrary",) * n_axes, vmem_limit_bytes=VMEM_LIMIT)


def _layer_norm(v, g, b):
    mu = jnp.mean(v, axis=-1, keepdims=True)
    d = v - mu
    var = jnp.mean(d * d, axis=-1, keepdims=True)
    return d * lax.rsqrt(var + LN_EPS) * g + b


def _rms_norm(v, g):
    ms = jnp.mean(v * v, axis=-1, keepdims=True)
    return v * lax.rsqrt(ms + RMS_EPS) * g


def _ln_kernel(x_ref, g_ref, b_ref, xf_ref, xb_ref):
    y = _layer_norm(x_ref[...], g_ref[...], b_ref[...])
    xf_ref[...] = y
    xb_ref[...] = y.astype(BF16)


def _input_ln(x, g, b, tm=512):
    L, D = x.shape
    row = pl.BlockSpec((tm, D), lambda i: (i, 0))
    vec = pl.BlockSpec((1, D), lambda i: (0, 0))
    return pl.pallas_call(
        _ln_kernel, grid=(L // tm,), in_specs=[row, vec, vec], out_specs=[row, row],
        out_shape=[jax.ShapeDtypeStruct((L, D), F32), jax.ShapeDtypeStruct((L, D), BF16)],
        compiler_params=_params(1), name="input_ln")(x, g, b)


def _norm_rope_heads(acc, g_ref, cos_ref, sa_ref, sb_ref):
    tm = acc.shape[0]
    gain = jnp.tile(g_ref[...], (1, tm // LANES))
    for h in range(acc.shape[1] // HEAD_DIM):
        sl = slice(h * HEAD_DIM, (h + 1) * HEAD_DIM)
        xt = acc[:, sl].T
        hn = xt * lax.rsqrt(jnp.mean(xt * xt, axis=0, keepdims=True) + RMS_EPS) * gain
        rot = pltpu.roll(hn, HEAD_DIM - 32, 0) * sa_ref[...] + pltpu.roll(hn, 32, 0) * sb_ref[...]
        yield sl, hn * cos_ref[...] + rot


def _inproj_q_kernel(x_ref, w_ref, g_ref, cos_ref, sa_ref, sb_ref, qt_ref, *, scale):
    for sl, r in _norm_rope_heads(_dot(x_ref[...], w_ref[...]), g_ref, cos_ref, sa_ref, sb_ref):
        qt_ref[sl, :] = (r * scale).astype(BF16)


def _inproj_kv_kernel(x_ref, wk_ref, wv_ref, g_ref, cos_ref, sa_ref, sb_ref, k_ref, vt_ref):
    x = x_ref[...]
    for sl, r in _norm_rope_heads(_dot(x, wk_ref[...]), g_ref, cos_ref, sa_ref, sb_ref):
        k_ref[:, sl] = r.T.astype(BF16)
    vt_ref[...] = _dot(x, wv_ref[...]).T.astype(BF16)


def _inproj_att(xb, w_in_b, l, qg, kg, tabs_t, tm=1024):
    L, K = xb.shape
    tn = KV_WIDTH
    tq = 2 * tn
    n_q = ATT_WIDTH // tn
    x_spec = pl.BlockSpec((tm, K), lambda i, j: (i, 0))
    w_tile = lambda width, col: pl.BlockSpec((None, K, width), lambda i, j: (l, 0, col(j)))
    gain = pl.BlockSpec((HEAD_DIM, LANES), lambda i, j: (0, 0))
    gain_col = lambda g: jnp.broadcast_to(g.reshape(HEAD_DIM, 1), (HEAD_DIM, LANES))
    tab_t = pl.BlockSpec((HEAD_DIM, tm), lambda i, j: (0, i))
    q_scale = HEAD_DIM ** -0.5 * math.log2(math.e)
    qt = pl.pallas_call(
        functools.partial(_inproj_q_kernel, scale=q_scale), grid=(L // tm, ATT_WIDTH // tq),
        in_specs=[x_spec, w_tile(tq, lambda j: j), gain, tab_t, tab_t, tab_t],
        out_specs=pl.BlockSpec((tq, tm), lambda i, j: (j, i)),
        out_shape=jax.ShapeDtypeStruct((ATT_WIDTH, L), BF16),
        compiler_params=_params(2), name="inproj_q")(xb, w_in_b, gain_col(qg), *tabs_t)
    k, vt = pl.pallas_call(
        _inproj_kv_kernel, grid=(L // tm, 1),
        in_specs=[x_spec, w_tile(tn, lambda j: n_q), w_tile(tn, lambda j: n_q + 1), gain, tab_t, tab_t, tab_t],
        out_specs=[pl.BlockSpec((tm, tn), lambda i, j: (i, 0)), pl.BlockSpec((tn, tm), lambda i, j: (0, i))],
        out_shape=[jax.ShapeDtypeStruct((L, KV_WIDTH), BF16), jax.ShapeDtypeStruct((KV_WIDTH, L), BF16)],
        compiler_params=_params(2), name="inproj_kv")(xb, w_in_b, w_in_b, gain_col(kg), *tabs_t)
    return qt, k, vt


def _attn_kernel(qt_ref, qtn_ref, *refs, tiles, ahead):
    k_first, k_next = refs[:ahead], refs[ahead:ahead + tiles]
    vt_ref, o_ref, m_ref, l_ref, acc_ref = refs[ahead + tiles:ahead + tiles + 5]
    s_refs = refs[ahead + tiles + 5:ahead + 2 * tiles + 5]
    p_refs = refs[ahead + 2 * tiles + 5:]
    i = pl.program_id(1)
    j = pl.program_id(2)
    last = j == pl.num_programs(2) - 1
    tk, tq = s_refs[0].shape[1:]
    sub = 8
    chunks = [pl.ds(c * ATT_CHUNK, ATT_CHUNK) for c in range(tk // ATT_CHUNK)]
    heads = range(GQA_GROUP)

    def scores(k_ref, s_ref, qt):
        k = k_ref[...]
        for h in heads:
            s_ref[h] = _dot(k, qt[h * HEAD_DIM:(h + 1) * HEAD_DIM, :])

    def softmax_pv(s_ref, p_ref, vt):
        for h in heads:
            m_part = jnp.full((sub, tq), -jnp.inf, F32)
            for rows in chunks:
                m_part = jnp.maximum(m_part, jnp.max(s_ref[h, rows, :].reshape(ATT_CHUNK // sub, sub, tq), axis=0))
            m_prev = m_ref[h]
            m_new = jnp.maximum(m_prev, jnp.max(m_part, axis=0, keepdims=True))
            alpha = jnp.exp2(m_prev - m_new)
            l_part = jnp.zeros((sub, tq), F32)
            for rows in chunks:
                p = jnp.exp2(s_ref[h, rows, :] - m_new)
                l_part = l_part + jnp.sum(p.reshape(ATT_CHUNK // sub, sub, tq), axis=0)
                p_ref[h, rows, :] = p.astype(BF16)
            l_ref[h] = alpha * l_ref[h] + jnp.sum(l_part, axis=0, keepdims=True)
            acc_ref[h] = alpha * acc_ref[h] + _dot(vt, p_ref[h])
            m_ref[h] = m_new

    @pl.when(j == 0)
    def _():
        m_ref[...] = jnp.full(m_ref.shape, -jnp.inf, F32)
        l_ref[...] = jnp.zeros(l_ref.shape, F32)
        acc_ref[...] = jnp.zeros(acc_ref.shape, F32)

    @pl.when(jnp.logical_and(j == 0, i == 0))
    def _():
        for r in range(ahead):
            scores(k_first[r], s_refs[r], qt_ref[...])

    qt = qt_ref[...]
    qt_next = jnp.where(last, qtn_ref[...], qt)
    for r in range(tiles):
        scores(k_next[r], s_refs[(r + ahead) % tiles], qt if r + ahead < tiles else qt_next)
        softmax_pv(s_refs[r], p_refs[r], vt_ref[:, r * tk:(r + 1) * tk])

    @pl.when(last)
    def _():
        for h in heads:
            o_ref[:, h * HEAD_DIM:(h + 1) * HEAD_DIM] = (acc_ref[h] / l_ref[h]).T.astype(BF16)


def _attention(qt, k, vt, tq=256, tk=1024, tiles=2, ahead=1):
    L = k.shape[0]
    gw = GQA_GROUP * HEAD_DIM
    nk = L // tk
    nq = L // tq
    k_tile = lambda f: pl.BlockSpec((tk, HEAD_DIM), lambda g, i, j: (f(j), g))
    k_first = [k_tile(functools.partial(lambda j, r: r, r=r)) for r in range(ahead)]
    k_next = [k_tile(functools.partial(lambda j, r: lax.rem(tiles * j + r + ahead, nk), r=r)) for r in range(tiles)]
    return pl.pallas_call(
        functools.partial(_attn_kernel, tiles=tiles, ahead=ahead), grid=(N_KV_HEADS, nq, nk // tiles),
        in_specs=[pl.BlockSpec((gw, tq), lambda g, i, j: (g, i)),
                  pl.BlockSpec((gw, tq), lambda g, i, j: (g, jnp.minimum(i + 1, nq - 1)))]
        + k_first + k_next + [pl.BlockSpec((HEAD_DIM, tiles * tk), lambda g, i, j: (g, j))],
        out_specs=pl.BlockSpec((tq, gw), lambda g, i, j: (i, g)),
        out_shape=jax.ShapeDtypeStruct((L, ATT_WIDTH), BF16),
        scratch_shapes=[pltpu.VMEM((GQA_GROUP, 1, tq), F32)] * 2 + [pltpu.VMEM((GQA_GROUP, HEAD_DIM, tq), F32)]
        + [pltpu.VMEM((GQA_GROUP, tk, tq), F32)] * tiles + [pltpu.VMEM((GQA_GROUP, tk, tq), BF16)] * tiles,
        compiler_params=_params(3), name="attention")(qt, qt, *([k] * (ahead + tiles)), vt)


HALO = BF16_SUBLANES


def _halo_specs(tm, K, L):
    hb = tm // HALO
    n_h = L // HALO
    return [pl.BlockSpec((tm, K), lambda i, j: (i, 0)),
            pl.BlockSpec((HALO, K), lambda i, j: (jnp.maximum(i * hb - 1, 0), 0)),
            pl.BlockSpec((HALO, K), lambda i, j: (jnp.minimum((i + 1) * hb, n_h - 1), 0))]


def _fill_extended(xe_ref, x_ref, xp_ref, xn_ref):
    tm = x_ref.shape[0]

    @pl.when(pl.program_id(1) == 0)
    def _():
        xe_ref[0:HALO, :] = xp_ref[...]
        xe_ref[HALO:HALO + tm, :] = x_ref[...]
        xe_ref[HALO + tm:, :] = xn_ref[...]


def _conv3_extended(u, w_ref, b_ref):
    n_ext = u.shape[0]
    tm = n_ext - 2 * HALO
    i = pl.program_id(0)
    top = u[:HALO] * jnp.where(i > 0, 1.0, 0.0)
    bot = u[HALO + tm:] * jnp.where(i < pl.num_programs(0) - 1, 1.0, 0.0)
    u = jnp.concatenate([top, u[HALO:HALO + tm], bot], axis=0)
    um = pltpu.roll(u, 1, 0)[HALO:HALO + tm]
    up = pltpu.roll(u, n_ext - 1, 0)[HALO:HALO + tm]
    return w_ref[0:1, :] * um + w_ref[1:2, :] * u[HALO:HALO + tm] + w_ref[2:3, :] * up + b_ref[...]


def _inproj_hy_kernel(x_ref, xp_ref, xn_ref, w0_ref, w1_ref, w2_ref, cw0_ref, cw1_ref, cw2_ref,
                      cb0_ref, cb1_ref, cb2_ref, z_ref, x0_ref, xe_ref):
    _fill_extended(xe_ref, x_ref, xp_ref, xn_ref)
    xe = xe_ref[...]
    x0_ref[...] = _conv3_extended(_dot(xe, w0_ref[...]), cw0_ref, cb0_ref).astype(BF16)
    x1 = _conv3_extended(_dot(xe, w1_ref[...]), cw1_ref, cb1_ref)
    v = _conv3_extended(_dot(xe, w2_ref[...]), cw2_ref, cb2_ref)
    z_ref[...] = (x1 * v).astype(BF16)


def _inproj_hy(xb, w_in_b, l, conv_w, conv_b, tm=1024, tn=512):
    L, K = xb.shape
    C = HYENA_WIDTH
    nj = C // tn
    w_specs = [pl.BlockSpec((None, K, tn), functools.partial(lambda i, j, g: (l, 0, (ATT_IN + g * C) // tn + j), g=g))
               for g in range(3)]
    cw_specs = [pl.BlockSpec((3, tn), functools.partial(lambda i, j, g: (0, g * nj + j), g=g)) for g in range(3)]
    cb_specs = [pl.BlockSpec((1, tn), functools.partial(lambda i, j, g: (0, g * nj + j), g=g)) for g in range(3)]
    out = pl.BlockSpec((tm, tn), lambda i, j: (i, j))
    return pl.pallas_call(
        _inproj_hy_kernel, grid=(L // tm, nj),
        in_specs=_halo_specs(tm, K, L) + w_specs + cw_specs + cb_specs,
        out_specs=[out, out], out_shape=[jax.ShapeDtypeStruct((L, C), BF16)] * 2,
        scratch_shapes=[pltpu.VMEM((tm + 2 * HALO, K), BF16)],
        compiler_params=_params(2), name="inproj_hy")(
            xb, xb, xb, w_in_b, w_in_b, w_in_b, conv_w, conv_w, conv_w, conv_b, conv_b, conv_b)


def _filt_feat_kernel(fb_ref, w1t_ref, w1c_ref, w1s_ref, b1_ref, f1_ref, w2_ref, b2_ref, f2_ref,
                      gf_ref, gr_ref, *, tm):
    pos = lax.broadcasted_iota(jnp.int32, (1, tm), 1) + pl.program_id(0) * tm

    def feats(p):
        pf = p.astype(F32)
        ang = fb_ref[...] * ((2.0 * math.pi * pf) / SEQ)
        pre = (w1t_ref[...] * (pf / (SEQ - 1)) + _dot_hi(w1c_ref[...], jnp.cos(ang))
               + _dot_hi(w1s_ref[...], -jnp.sin(ang)) + b1_ref[...])
        h = jnp.sin(f1_ref[...] * pre)
        return jnp.sin(f2_ref[...] * (_dot_hi(w2_ref[...], h) + b2_ref[...]))

    gf_ref[...] = feats(pos)
    gr_ref[...] = feats(jnp.where(pos == 0, 0, SEQ - pos))


def _filt_feat(fband, w1, b1, f1, w2, b2, f2, tm=1024):
    H = FILTER_HIDDEN
    full = lambda a: pl.BlockSpec(a.shape, lambda i: (0,) * a.ndim)
    col = lambda v: v[:, None]
    args = [col(fband), col(w1[0]), w1[1:1 + FILTER_BANDS].T, w1[1 + FILTER_BANDS:].T, col(b1), col(f1),
            w2.T, col(b2), col(f2)]
    out = pl.BlockSpec((H, tm), lambda i: (0, i))
    return pl.pallas_call(
        functools.partial(_filt_feat_kernel, tm=tm), grid=(SEQ // tm,),
        in_specs=[full(a) for a in args], out_specs=[out, out],
        out_shape=[jax.ShapeDtypeStruct((H, SEQ), F32)] * 2,
        compiler_params=_params(1), name="filt_feat")(*args)


def _dot_t(a_t, b):
    return lax.dot_general(a_t.astype(BF16), b.astype(BF16), (((0,), (0,)), ((), ())), preferred_element_type=F32)


def _filt_gen_kernel(gf_ref, gr_ref, w3f_ref, w3b_ref, b3f_ref, b3b_ref, df_ref, db_ref,
                     kf_ref, kb_ref, norm_ref, *, tm):
    i = pl.program_id(0)
    pos = lax.broadcasted_iota(jnp.int32, (tm, HYENA_WIDTH), 0) + i * tm
    pos_r = jnp.where(pos == 0, 0, SEQ - pos)
    t = pos.astype(F32) / (SEQ - 1)
    tr = pos_r.astype(F32) / (SEQ - 1)
    hf = (_dot_t(gf_ref[...], w3f_ref[...]) + b3f_ref[...]) * jnp.exp(-t * jnp.abs(df_ref[...]))
    hb = (_dot_t(gr_ref[...], w3b_ref[...]) + b3b_ref[...]) * jnp.exp(-tr * jnp.abs(db_ref[...]))

    @pl.when(i == 0)
    def _():
        norm_ref[...] = jnp.zeros(norm_ref.shape, F32)

    norm_ref[...] += jnp.sum(jnp.abs(hf) + jnp.abs(hb), axis=0, keepdims=True)
    kf_ref[...] = hf.astype(BF16)
    kb_ref[...] = jnp.where(pos == 0, 0.0, -hb).astype(BF16)


def _filt_gen(gf, gr, w3, b3, decay, tm=512):
    C, H = HYENA_WIDTH, FILTER_HIDDEN
    g = pl.BlockSpec((H, tm), lambda i: (0, i))
    w = pl.BlockSpec((H, C), lambda i: (0, 0))
    vec = pl.BlockSpec((1, C), lambda i: (0, 0))
    out = pl.BlockSpec((tm, C), lambda i: (i, 0))
    return pl.pallas_call(
        functools.partial(_filt_gen_kernel, tm=tm), grid=(SEQ // tm,),
        in_specs=[g, g, w, w, vec, vec, vec, vec], out_specs=[out, out, vec],
        out_shape=[jax.ShapeDtypeStruct((SEQ, C), BF16)] * 2 + [jax.ShapeDtypeStruct((1, C), F32)],
        compiler_params=_params(1), name="filt_gen")(
            gf, gr, w3[:, :C], w3[:, C:], b3[None, :C], b3[None, C:], decay[0:1], decay[1:2])


def _fft_tables():
    N = FFT_N
    n2 = np.arange(FFT_R, dtype=np.int64)[:, None, None]
    k1 = np.arange(FFT_R // 2, dtype=np.int64)[None, :, None]
    n1 = np.arange(FFT_R, dtype=np.int64)[None, None, :]
    theta = (((FFT_R * n1 + n2) * (2 * k1 + 1)) % (2 * N)) * (math.pi / N)
    c, s = np.cos(theta), np.sin(theta)
    ta = np.concatenate([c, -s], axis=1)
    ti = np.transpose(ta[:, :, :FFT_R // 2], (0, 2, 1)) * (2.0 / N)
    j = np.arange(FFT_R, dtype=np.int64)
    phi = ((j[:, None] * j[None, :]) % FFT_R) * (2.0 * math.pi / FFT_R)
    cm, sm = np.cos(phi), np.sin(phi)
    f2f = np.block([[cm, sm], [-sm, cm]])
    f2i = np.block([[cm, -sm], [sm, cm]])
    return tuple(jnp.asarray(t.astype(np.float32)).astype(BF16) for t in (ta, ti, f2f, f2i))


def _fft_conv_kernel(z_ref, kf_ref, kb_ref, ta_ref, ti_ref, f2f_ref, f2i_ref, y_ref, pz_ref, pk_ref, s_ref):
    half = FFT_R // 2
    tc = z_ref.shape[1]
    zeros_hi = jnp.zeros((half, tc), BF16)

    for c in range(FFT_R // N2_CHUNK):
        off = c * N2_CHUNK
        for n1 in range(half):
            pz_ref[pl.ds(n1 * P_PITCH, N2_CHUNK), :] = z_ref[pl.ds(n1 * FFT_R + off, N2_CHUNK), :].astype(F32)
            pk_ref[pl.ds(n1 * P_PITCH, N2_CHUNK), :] = kf_ref[pl.ds(n1 * FFT_R + off, N2_CHUNK), :].astype(F32)
            pk_ref[pl.ds((half + n1) * P_PITCH, N2_CHUNK), :] = (
                kb_ref[pl.ds(n1 * FFT_R + off, N2_CHUNK), :].astype(F32))

        def stage_a(j, carry):
            n2 = off + j
            xz = pz_ref[pl.ds(j, half, stride=P_PITCH), :].astype(BF16)
            xk = pk_ref[pl.ds(j, FFT_R, stride=P_PITCH), :].astype(BF16)
            rhs = jnp.concatenate([jnp.concatenate([xz, zeros_hi], axis=0), xk], axis=1)
            a = _dot(ta_ref[n2], rhs)
            row = pl.multiple_of(n2 * S_PITCH, 8)
            s_ref[0, pl.ds(row, FFT_R), :] = a[:, :tc]
            s_ref[1, pl.ds(row, FFT_R), :] = a[:, tc:]
            return carry

        lax.fori_loop(0, N2_CHUNK, stage_a, 0, unroll=FFT_UNROLL)

    def rows(k1):
        return pl.ds(k1, FFT_R, stride=S_PITCH), pl.ds(half + k1, FFT_R, stride=S_PITCH)

    def spectrum(k1):
        re_rows, im_rows = rows(k1)
        rhs = jnp.concatenate(
            [jnp.concatenate([s_ref[0, re_rows, :], s_ref[1, re_rows, :]], axis=1),
             jnp.concatenate([s_ref[0, im_rows, :], s_ref[1, im_rows, :]], axis=1)], axis=0).astype(BF16)
        return _dot(f2f_ref[...], rhs)

    def product(x):
        zr, zi = x[:FFT_R, :tc], x[FFT_R:, :tc]
        kr, ki = x[:FFT_R, tc:], x[FFT_R:, tc:]
        return jnp.concatenate([zr * kr - zi * ki, zr * ki + zi * kr], axis=0).astype(BF16)

    def stage_b(i, carry):
        k1s = [i * FFT_KB + u for u in range(FFT_KB)]
        xs = [spectrum(k1) for k1 in k1s]
        for u in range(0, FFT_KB, 2):
            d = _dot(f2i_ref[...], jnp.concatenate([product(xs[u]), product(xs[u + 1])], axis=1))
            for v in range(2):
                re_rows, im_rows = rows(k1s[u + v])
                s_ref[0, re_rows, :] = d[:FFT_R, v * tc:(v + 1) * tc]
                s_ref[0, im_rows, :] = d[FFT_R:, v * tc:(v + 1) * tc]
        return carry

    lax.fori_loop(0, half // FFT_KB, stage_b, 0)

    def stage_out(n2, carry):
        d = s_ref[0, pl.ds(pl.multiple_of(n2 * S_PITCH, 8), FFT_R), :].astype(BF16)
        y_ref[pl.ds(n2, half, stride=FFT_R), :] = _dot(ti_ref[n2], d)
        return carry

    lax.fori_loop(0, FFT_R, stage_out, 0, unroll=FFT_UNROLL)


def _fft_conv(z, kf, kb, tables):
    L, C = z.shape
    tc = FFT_TC
    ta, ti, f2f, f2i = tables
    col = pl.BlockSpec((L, tc), lambda j: (0, j))
    full = lambda a: pl.BlockSpec(a.shape, lambda j: (0,) * a.ndim)
    half = FFT_R // 2
    return pl.pallas_call(
        _fft_conv_kernel, grid=(C // tc,),
        in_specs=[col, col, col, full(ta), full(ti), full(f2f), full(f2i)],
        out_specs=col, out_shape=jax.ShapeDtypeStruct((L, C), F32),
        scratch_shapes=[pltpu.VMEM((half * P_PITCH, tc), F32), pltpu.VMEM((FFT_R * P_PITCH, tc), F32),
                        pltpu.VMEM((2, FFT_R * S_PITCH, tc), F32)],
        compiler_params=_params(1), name="fft_conv")(z, kf, kb, ta, ti, f2f, f2i)


def _outproj_kernel(a_ref, x0_ref, z_ref, yc_ref, norm_ref, skip_ref, ag_ref, hg_ref,
                    w_ref, b_ref, x_ref, g_ref, beta_ref, xf_ref, xb_ref, *, sub_rows):
    inv_norm = 1.0 / norm_ref[...]
    for r in range(a_ref.shape[0] // sub_rows):
        rows = pl.ds(r * sub_rows, sub_rows)
        a = _rms_norm(a_ref[rows, :].astype(F32), ag_ref[...])
        z = z_ref[rows, :].astype(F32)
        hy = x0_ref[rows, :].astype(F32) * (yc_ref[rows, :] * inv_norm + skip_ref[...] * z)
        h = _rms_norm(hy, hg_ref[...])
        mix = _dot(jnp.concatenate([a.astype(BF16), h.astype(BF16)], axis=1), w_ref[...]) + b_ref[...]
        y = _layer_norm(ALPHA * x_ref[rows, :] + mix, g_ref[...], beta_ref[...])
        xf_ref[rows, :] = y
        xb_ref[rows, :] = y.astype(BF16)


def _outproj(a, x0, z, yc, norm, skip, ag, hg, w_out, l, b_out, x, g, beta, tm=512, sub_rows=128):
    L, D = x.shape
    C = HYENA_WIDTH
    half = pl.BlockSpec((tm, C), lambda i: (i, 0))
    hvec = pl.BlockSpec((1, C), lambda i: (0, 0))
    row = pl.BlockSpec((tm, D), lambda i: (i, 0))
    vec = pl.BlockSpec((1, D), lambda i: (0, 0))
    w = pl.BlockSpec((None, ATT_WIDTH + C, D), lambda i: (l, 0, 0), pipeline_mode=pl.Buffered(1))
    return pl.pallas_call(
        functools.partial(_outproj_kernel, sub_rows=sub_rows), grid=(L // tm,),
        in_specs=[half, half, half, half, hvec, hvec, hvec, hvec, w, vec, row, vec, vec],
        out_specs=[row, row],
        out_shape=[jax.ShapeDtypeStruct((L, D), F32), jax.ShapeDtypeStruct((L, D), BF16)],
        compiler_params=_params(1), name="outproj_ln")(
            a, x0, z, yc, norm, skip, ag, hg, w_out, b_out, x, g, beta)


def _gelu_tanh(x):
    return x * (0.5 * (1.0 + jnp.tanh(math.sqrt(2.0 / math.pi) * (x + 0.044715 * (x * x * x)))))


def _ffn_up_kernel(x_ref, xp_ref, xn_ref, wg_ref, *refs):
    wv_refs, (bg_ref, bv_ref, cw_ref, cb_ref, h_ref, xe_ref) = refs[:-6], refs[-6:]
    _fill_extended(xe_ref, x_ref, xp_ref, xn_ref)
    gate = _conv3_extended(_dot(xe_ref[...], wg_ref[...].astype(BF16)) + bg_ref[...], cw_ref, cb_ref)
    val = _dot(x_ref[...], jnp.concatenate([r[...].astype(BF16) for r in wv_refs], axis=1)) + bv_ref[...]
    h_ref[...] = (_gelu_tanh(gate) * val).astype(BF16)


def _ffn_up(xb, w_up_b, b_gate, b_val, conv_w, conv_b, l, tm=1024, tn=512):
    L, K = xb.shape
    n_lane_blocks = w_up_b.shape[2] // LANES
    val0 = D_FF // LANES
    w_val = [pl.BlockSpec((None, K, LANES), functools.partial(
        lambda i, j, r: (l, 0, jnp.minimum(val0 + (tn // LANES) * j + r, n_lane_blocks - 1)), r=r))
        for r in range(tn // LANES)]
    vec = pl.BlockSpec((1, tn), lambda i, j: (0, j))
    return pl.pallas_call(
        _ffn_up_kernel, grid=(L // tm, pl.cdiv(D_FF, tn)),
        in_specs=_halo_specs(tm, K, L) + [pl.BlockSpec((None, K, tn), lambda i, j: (l, 0, j))] + w_val
        + [vec, vec, pl.BlockSpec((3, tn), lambda i, j: (0, j)), vec],
        out_specs=pl.BlockSpec((tm, tn), lambda i, j: (i, j)),
        out_shape=jax.ShapeDtypeStruct((L, D_FF), BF16),
        scratch_shapes=[pltpu.VMEM((tm + 2 * HALO, K), BF16)],
        compiler_params=_params(2), name="ffn_up")(
            xb, xb, xb, w_up_b, *([w_up_b] * (tn // LANES)), b_gate, b_val, conv_w, conv_b)


def _ffn_down_kernel(h_ref, w_ref, b_ref, x_ref, g_ref, beta_ref, xf_ref, xb_ref):
    ffn = _dot(h_ref[...], w_ref[...]) + b_ref[...]
    y = _layer_norm(ALPHA * x_ref[...] + ffn, g_ref[...], beta_ref[...])
    xf_ref[...] = y
    xb_ref[...] = y.astype(BF16)


def _ffn_down(h, w_down, b_down, x, g, beta, l, tm=256):
    L, D = x.shape
    row = pl.BlockSpec((tm, D), lambda i: (i, 0))
    vec = pl.BlockSpec((1, D), lambda i: (0, 0))
    return pl.pallas_call(
        _ffn_down_kernel, grid=(L // tm,),
        in_specs=[pl.BlockSpec((tm, D_FF), lambda i: (i, 0)),
                  pl.BlockSpec((None, D_FF, D), lambda i: (l, 0, 0), pipeline_mode=pl.Buffered(1)),
                  vec, row, vec, vec],
        out_specs=[row, row],
        out_shape=[jax.ShapeDtypeStruct((L, D), F32), jax.ShapeDtypeStruct((L, D), BF16)],
        compiler_params=_params(1), name="ffn_down_ln")(h, w_down, b_down, x, g, beta)


def _rope_tables(L):
    rows = L // GRID_W
    row_pos = np.repeat(np.arange(rows, dtype=np.float64), GRID_W)
    col_pos = np.tile(np.arange(GRID_W, dtype=np.float64), rows)
    dims = HEAD_DIM // 2

    def axis_table(pos):
        inv = ROPE_THETA ** (-np.arange(0, dims, 2, dtype=np.float64) / dims)
        ang = pos[:, None] * inv[None, :]
        ang = np.concatenate([ang, ang], axis=-1)
        return np.cos(ang), np.sin(ang)

    cos_r, sin_r = axis_table(row_pos)
    cos_c, sin_c = axis_table(col_pos)
    cos = np.concatenate([cos_r, cos_c], axis=-1)
    sin = np.concatenate([sin_r, sin_c], axis=-1)
    first = (np.arange(HEAD_DIM) % dims) < dims // 2
    tabs = (cos, np.where(first, -sin, 0.0), np.where(first, 0.0, sin))
    return tuple(jnp.asarray(np.ascontiguousarray(t.T.astype(np.float32))) for t in tabs)


def kernel(x, ln_in_g, ln_in_b, w_in, q_norm_g, k_norm_g, hy_conv_w, hy_conv_b, filt_w1, filt_b1, filt_f1, filt_w2, filt_b2, filt_f2, filt_w3, filt_b3, filt_decay, hy_skip, att_out_g, hy_out_g, w_out, b_out, ln1_g, ln1_b, w_up, b_up, ffn_conv_w, ffn_conv_b, w_down, b_down, ln2_g, ln2_b):
    assert x.shape == (1, SEQ, D_MODEL) and w_in.shape == (DEPTH, D_MODEL, ATT_IN + HY_IN)
    rope_t = _rope_tables(SEQ)
    tables = _fft_tables()
    fband = jnp.linspace(1e-4, FILTER_BANDS - 1, FILTER_BANDS, dtype=F32)

    w_in_b = w_in.astype(BF16)
    w_out_b = w_out.astype(BF16)
    w_down_b = w_down.astype(BF16)

    xf, xb = _input_ln(x[0], ln_in_g[None], ln_in_b[None])
    for l in range(DEPTH):
        qt, k, vt = _inproj_att(xb, w_in_b, l, q_norm_g[l], k_norm_g[l], rope_t)
        att = _attention(qt, k, vt)
        z, x0 = _inproj_hy(xb, w_in_b, l, hy_conv_w[l], hy_conv_b[l][None])
        gf, gr = _filt_feat(fband, filt_w1[l], filt_b1[l], filt_f1[l], filt_w2[l], filt_b2[l], filt_f2[l])
        kf, kb, norm = _filt_gen(gf, gr, filt_w3[l], filt_b3[l], filt_decay[l])
        yc = _fft_conv(z, kf, kb, tables)
        xf, xb = _outproj(att, x0, z, yc, norm, hy_skip[l][None], att_out_g[l][None], hy_out_g[l][None],
                          w_out_b, l, b_out[l][None], xf, ln1_g[l][None], ln1_b[l][None])
        h = _ffn_up(xb, w_up, b_up[l][None, :D_FF], b_up[l][None, D_FF:],
                    ffn_conv_w[l], ffn_conv_b[l][None], l)
        xf, xb = _ffn_down(h, w_down_b, b_down[l][None], xf, ln2_g[l][None], ln2_b[l][None], l)
    return xf[None]
```
